```python
import math
import jax, jax.numpy as jnp
from jax import lax
import numpy as np

D_MODEL = 1024
BATCH = 8
SEQ = 2048
DEPTH = 1

CHUNK = 64
N_MEM = 256
EPS = 1e-6

ATT_HEADS = 8
ATT_KV_HEADS = 2
ATT_HEAD_DIM = 64
ATT_GROUP = ATT_HEADS // ATT_KV_HEADS
WINDOW = 128
LOOKBACK = WINDOW // CHUNK
ATT_Q_W = ATT_HEADS * ATT_HEAD_DIM
ATT_KV_W = ATT_KV_HEADS * ATT_HEAD_DIM

HG_HEADS = 4
HG_KEY_DIM = 128
HG_VAL_DIM = 128
HG_K_W = HG_HEADS * HG_KEY_DIM
HG_V_W = HG_HEADS * HG_VAL_DIM

X_HEADS = 4
X_HEAD_DIM = D_MODEL // X_HEADS

D_FF = 2816
CONV_WIDTH = 3

IN_SPLITS = (ATT_Q_W, ATT_KV_W, ATT_KV_W, HG_K_W, HG_K_W, HG_V_W, HG_V_W, D_MODEL, D_MODEL)
IN_W = ATT_Q_W + 2 * ATT_KV_W + 2 * HG_K_W + 2 * HG_V_W + 2 * D_MODEL

kernel_name = "hybrid_swa_sink_hgrn2_gated_merge"


def rms_norm(x, gain):
    xf = x.astype(jnp.float32)
    y = xf * lax.rsqrt(jnp.mean(xf * xf, axis=-1, keepdims=True) + EPS)
    return (y * gain.astype(jnp.float32)).astype(x.dtype)


def alibi_slopes(n_heads):
    return 2.0 ** (-8.0 * jnp.arange(1, n_heads + 1, dtype=jnp.float32) / n_heads)


def swa_sink_attention(q, k, v, sinks):
    B, T = q.shape[0], q.shape[1]
    N = T // CHUNK
    KW = (LOOKBACK + 1) * CHUNK
    G, R, Dh = ATT_KV_HEADS, ATT_GROUP, ATT_HEAD_DIM
    qb = q.reshape(B, N, CHUNK, G, R, Dh)
    pad = ((0, 0), (LOOKBACK * CHUNK, 0), (0, 0))
    kp = jnp.pad(k, pad).reshape(B, N + LOOKBACK, CHUNK, G, Dh)
    vp = jnp.pad(v, pad).reshape(B, N + LOOKBACK, CHUNK, G, Dh)
    kb = jnp.concatenate([kp[:, j:j + N] for j in range(LOOKBACK + 1)], axis=2)
    vb = jnp.concatenate([vp[:, j:j + N] for j in range(LOOKBACK + 1)], axis=2)
    s = jnp.einsum('bncgrd,bnkgd->bngrck', qb, kb).astype(jnp.float32) * (Dh ** -0.5)
    dist = (jnp.arange(CHUNK)[:, None] + LOOKBACK * CHUNK) - jnp.arange(KW)[None, :]
    bias = -alibi_slopes(ATT_HEADS).reshape(G, R)[:, :, None, None] * jnp.abs(dist).astype(jnp.float32)
    k_pos = (jnp.arange(N)[:, None] - LOOKBACK) * CHUNK + jnp.arange(KW)[None, :]
    valid = (k_pos >= 0)[:, None, None, None, :]
    s = jnp.where(valid, s + bias, -jnp.inf)
    sink = sinks.astype(jnp.float32).reshape(G, R)[:, :, None, None]
    m = jnp.maximum(jnp.max(s, axis=-1, keepdims=True), sink)
    p = jnp.exp(s - m)
    probs = p / (jnp.sum(p, axis=-1, keepdims=True) + jnp.exp(sink - m))
    o = jnp.einsum('bngrck,bnkgd->bncgrd', probs.astype(v.dtype), vb)
    return o.reshape(B, T, ATT_Q_W)


def hgrn2_chunkwise(q, f_logit, i, lb):
    B, T = q.shape[0], q.shape[1]
    N = T // CHUNK
    f32 = jnp.float32
    lb = lb.astype(f32)
    f = lb + (1.0 - lb) * jax.nn.sigmoid(f_logit.astype(f32))
    shp_k = (B, N, CHUNK, HG_HEADS, HG_KEY_DIM)
    qc = q.astype(f32).reshape(shp_k) * (HG_KEY_DIM ** -0.5)
    kc = (1.0 - f).reshape(shp_k)
    vc = i.astype(f32).reshape(B, N, CHUNK, HG_HEADS, HG_VAL_DIM)
    b = jnp.cumsum(jnp.log(f).reshape(shp_k), axis=2)
    b_last = b[:, :, -1:]
    q_dec = qc * jnp.exp(b)
    k_inv = kc * jnp.exp(-b)
    k_end = kc * jnp.exp(b_last - b)
    causal = jnp.tril(jnp.ones((CHUNK, CHUNK), dtype=bool))
    a = jnp.where(causal, jnp.einsum('bnchd,bnshd->bnhcs', q_dec, k_inv), 0.0)
    o_intra = jnp.einsum('bnhcs,bnshv->bnchv', a, vc)
    ds = jnp.einsum('bnshd,bnshv->bnhdv', k_end, vc)
    decay = jnp.exp(b_last[:, :, 0])

    def step(state, inp):
        dec, d = inp
        return dec[..., None] * state + d, state

    s0 = jnp.zeros((B, HG_HEADS, HG_KEY_DIM, HG_VAL_DIM), f32)
    _, s_in = lax.scan(step, s0, (jnp.moveaxis(decay, 1, 0), jnp.moveaxis(ds, 1, 0)))
    s_in = jnp.moveaxis(s_in, 0, 1)
    o_inter = jnp.einsum('bnchd,bnhdv->bnchv', q_dec, s_in)
    return (o_intra + o_inter).reshape(B, T, HG_HEADS, HG_VAL_DIM)


def gated_head_rms_norm(o, gate, gain):
    B, T = o.shape[0], o.shape[1]
    y = o * lax.rsqrt(jnp.mean(o * o, axis=-1, keepdims=True) + EPS) * gain.astype(jnp.float32)
    y = y.reshape(B, T, HG_V_W) * jax.nn.silu(gate.astype(jnp.float32))
    return y.astype(gate.dtype)


def hybrid_mixer(h, g_mix, w_in, lb, sinks, g_onorm, w_branch_a, w_branch_b, w_mix_out):
    n = rms_norm(h, g_mix)
    proj = n @ w_in
    points = np.cumsum(IN_SPLITS)[:-1].tolist()
    aq, ak, av, hq, hf, hi, hg, gate_a, gate_b = jnp.split(proj, points, axis=-1)
    ya = swa_sink_attention(aq, ak, av, sinks) @ w_branch_a
    yb = gated_head_rms_norm(hgrn2_chunkwise(hq, hf, hi, lb), hg, g_onorm) @ w_branch_b
    y = jax.nn.sigmoid(gate_a) * ya + jax.nn.sigmoid(gate_b) * yb
    return y @ w_mix_out


def memory_cross_attention(h, mem, g_cross, g_mem, w_cq, w_ckv, w_co):
    B, T = h.shape[0], h.shape[1]
    M = mem.shape[1]
    nx = rms_norm(h, g_cross)
    nm = rms_norm(mem, g_mem)
    q = (nx @ w_cq).reshape(B, T, X_HEADS, X_HEAD_DIM)
    k, v = jnp.split(nm @ w_ckv, 2, axis=-1)
    k = k.reshape(B, M, X_HEADS, X_HEAD_DIM)
    v = v.reshape(B, M, X_HEADS, X_HEAD_DIM)
    s = jnp.einsum('bthd,bmhd->bhtm', q, k).astype(jnp.float32) * (X_HEAD_DIM ** -0.5)
    p = jax.nn.softmax(s, axis=-1)
    o = jnp.einsum('bhtm,bmhd->bthd', p.astype(v.dtype), v).reshape(B, T, D_MODEL)
    return o @ w_co


def conv_gated_ffn(h, g_ffn, w_ffn_in, conv_w, conv_b, w_ffn_down):
    n = rms_norm(h, g_ffn)
    u, gate = jnp.split(n @ w_ffn_in, 2, axis=-1)
    u = lax.conv_general_dilated(
        u, conv_w[:, None, :].astype(u.dtype), window_strides=(1,),
        padding=[(CONV_WIDTH - 1, 0)], dimension_numbers=('NWC', 'WIO', 'NWC'),
        feature_group_count=D_FF) + conv_b
    return (jax.nn.silu(u) * gate) @ w_ffn_down


def setup_inputs(seed: int = 0) -> dict:
    key = jax.random.key(seed)
    ks = jax.random.split(key, 24)
    f32 = jnp.float32

    def w(k, shape, fan_in):
        return jax.random.normal(k, shape, f32) * (fan_in ** -0.5)

    def gain(k, shape):
        return 1.0 + 0.05 * jax.random.normal(k, shape, f32)

    L = DEPTH
    return {
        'x': jax.random.normal(ks[0], (BATCH, SEQ, D_MODEL), f32),
        'mem': jax.random.normal(ks[1], (BATCH, N_MEM, D_MODEL), f32),
        'g_mix': gain(ks[2], (L, D_MODEL)),
        'w_in': w(ks[3], (L, D_MODEL, IN_W), D_MODEL),
        'lower_bounds': 1.0 + 0.1 * jax.random.normal(ks[4], (DEPTH + 1, HG_K_W), f32),
        'attn_sinks': 0.5 * jax.random.normal(ks[5], (L, ATT_HEADS), f32),
        'g_onorm': gain(ks[6], (L, HG_VAL_DIM)),
        'w_branch_a': w(ks[7], (L, ATT_Q_W, D_MODEL), ATT_Q_W),
        'w_branch_b': w(ks[8], (L, HG_V_W, D_MODEL), HG_V_W),
        'w_mix_out': w(ks[9], (L, D_MODEL, D_MODEL), D_MODEL),
        'g_cross': gain(ks[10], (L, D_MODEL)),
        'g_mem': gain(ks[11], (L, D_MODEL)),
        'w_cq': w(ks[12], (L, D_MODEL, D_MODEL), D_MODEL),
        'w_ckv': w(ks[13], (L, D_MODEL, 2 * D_MODEL), D_MODEL),
        'w_co': w(ks[14], (L, D_MODEL, D_MODEL), D_MODEL),
        'g_ffn': gain(ks[15], (L, D_MODEL)),
        'w_ffn_in': w(ks[16], (L, D_MODEL, 2 * D_FF), D_MODEL),
        'conv_w': w(ks[17], (L, CONV_WIDTH, D_FF), CONV_WIDTH),
        'conv_b': 0.02 * jax.random.normal(ks[18], (L, D_FF), f32),
        'w_ffn_down': w(ks[19], (L, D_FF, D_MODEL), D_FF),
        'g_final': gain(ks[20], (D_MODEL,)),
    }


def reference(x, mem, g_mix, w_in, lower_bounds, attn_sinks, g_onorm, w_branch_a, w_branch_b,
              w_mix_out, g_cross, g_mem, w_cq, w_ckv, w_co, g_ffn, w_ffn_in, conv_w, conv_b,
              w_ffn_down, g_final):
    lb_all = jnp.cumsum(jax.nn.softmax(lower_bounds.astype(jnp.float32), axis=0), axis=0)
    h = x
    for l in range(DEPTH):
        h = h + hybrid_mixer(h, g_mix[l], w_in[l], lb_all[l], attn_sinks[l], g_onorm[l],
                             w_branch_a[l], w_branch_b[l], w_mix_out[l])
        h = h + memory_cross_attention(h, mem, g_cross[l], g_mem[l], w_cq[l], w_ckv[l], w_co[l])
        h = h + conv_gated_ffn(h, g_ffn[l], w_ffn_in[l], conv_w[l], conv_b[l], w_ffn_down[l])
    return rms_norm(h, g_final)
```

```python
import functools

import jax
import jax.numpy as jnp
from jax import lax
from jax.experimental import pallas as pl
from jax.experimental.pallas import tpu as pltpu

D_MODEL = 1024
CHUNK = 64
EPS = 1e-6

ATT_HEADS = 8
ATT_KV_HEADS = 2
ATT_HEAD_DIM = 64
ATT_GROUP = ATT_HEADS // ATT_KV_HEADS
LOOKBACK = 2
ATT_Q_W = ATT_HEADS * ATT_HEAD_DIM
ATT_KV_W = ATT_KV_HEADS * ATT_HEAD_DIM
KV_HALO = LOOKBACK * CHUNK
KWIN = (LOOKBACK + 1) * CHUNK

HG_HEADS = 4
HG_DIM = 128
HG_W = HG_HEADS * HG_DIM

X_HEADS = 4
X_HEAD_DIM = D_MODEL // X_HEADS

D_FF = 2816
CONV_WIDTH = 3
CONV_HALO = 8

_OFF_AQ = 0
_OFF_AK = _OFF_AQ + ATT_Q_W
_OFF_AV = _OFF_AK + ATT_KV_W
_OFF_HQ = _OFF_AV + ATT_KV_W
_OFF_HF = _OFF_HQ + HG_W
_OFF_HI = _OFF_HF + HG_W
_OFF_HG = _OFF_HI + HG_W
_OFF_GA = _OFF_HG + HG_W
_OFF_GB = _OFF_GA + D_MODEL
IN_W = _OFF_GB + D_MODEL

TOKEN_TILE = 256
VMEM_LIMIT_BYTES = 56 * 1024 * 1024

_F32 = jnp.float32
_BF16 = jnp.bfloat16
_NT = (((1,), (1,)), ((), ()))


def _rms(xf, gain):
    return xf * lax.rsqrt(jnp.mean(xf * xf, axis=-1, keepdims=True) + EPS) * gain


def _dot(a, b):
    return jnp.dot(a, b, preferred_element_type=_F32)


def _dot_nt(a, b):
    return lax.dot_general(a, b, _NT, preferred_element_type=_F32)


def _split3_bf16(v):
    hi = v.astype(_BF16)
    r1 = v - hi.astype(_F32)
    mid = r1.astype(_BF16)
    lo = (r1 - mid.astype(_F32)).astype(_BF16)
    return hi, mid, lo


def _mixer_kernel(x_ref, gmix_ref, win_ref, lbraw_ref, sinks_ref, gon_ref, wa_ref, wb_ref,
                  wmix_ref, o_ref, proj_ref, kbuf_ref, vbuf_ref, st_ref, ao_ref, hy_ref, *, tt):
    t = pl.program_id(1)

    @pl.when(t == 0)
    def _():
        kbuf_ref[0:KV_HALO, :] = jnp.zeros((KV_HALO, ATT_KV_W), _BF16)
        vbuf_ref[0:KV_HALO, :] = jnp.zeros((KV_HALO, ATT_KV_W), _BF16)
        st_ref[...] = jnp.zeros(st_ref.shape, _F32)

    n = _rms(x_ref[...], gmix_ref[...]).astype(_BF16)
    proj_ref[...] = _dot(n, win_ref[...])
    kbuf_ref[KV_HALO:KV_HALO + tt, :] = proj_ref[:, _OFF_AK:_OFF_AK + ATT_KV_W].astype(_BF16)
    vbuf_ref[KV_HALO:KV_HALO + tt, :] = proj_ref[:, _OFF_AV:_OFF_AV + ATT_KV_W].astype(_BF16)

    lraw = lbraw_ref[...]
    lmax = jnp.max(lraw, axis=0, keepdims=True)
    lexp = jnp.exp(lraw - lmax)
    lb = lexp[0:1, :] / jnp.sum(lexp, axis=0, keepdims=True)

    row = lax.broadcasted_iota(jnp.int32, (CHUNK, KWIN), 0)
    col = lax.broadcasted_iota(jnp.int32, (CHUNK, KWIN), 1)
    absdist = jnp.abs(row + KV_HALO - col).astype(_F32)
    crow = lax.broadcasted_iota(jnp.int32, (CHUNK, CHUNK), 0)
    ccol = lax.broadcasted_iota(jnp.int32, (CHUNK, CHUNK), 1)
    causal = crow >= ccol
    tri = causal.astype(_BF16)
    gon = gon_ref[...]

    def chunk_body(c, carry):
        r0 = pl.multiple_of(c * CHUNK, CHUNK)
        rows = pl.ds(r0, CHUNK)

        q = proj_ref[rows, _OFF_AQ:_OFF_AQ + ATT_Q_W]
        kw = kbuf_ref[pl.ds(r0, KWIN), :]
        vw = vbuf_ref[pl.ds(r0, KWIN), :]
        valid = (t * tt + r0 - KV_HALO + col) >= 0
        heads = []
        for h in range(ATT_HEADS):
            g = h // ATT_GROUP
            gs = slice(g * ATT_HEAD_DIM, (g + 1) * ATT_HEAD_DIM)
            qh = q[:, h * ATT_HEAD_DIM:(h + 1) * ATT_HEAD_DIM].astype(_BF16)
            s = _dot_nt(qh, kw[:, gs]) * (ATT_HEAD_DIM ** -0.5) - (2.0 ** -(h + 1)) * absdist
            s = jnp.where(valid, s, -jnp.inf)
            sink = sinks_ref[h]
            m = jnp.maximum(jnp.max(s, axis=-1, keepdims=True), sink)
            p = jnp.exp(s - m)
            den = jnp.sum(p, axis=-1, keepdims=True) + jnp.exp(sink - m)
            heads.append(_dot(p.astype(_BF16), vw[:, gs]) / den)
        ao_ref[rows, :] = jnp.concatenate(heads, axis=-1).astype(_BF16)

        hq = proj_ref[rows, _OFF_HQ:_OFF_HQ + HG_W]
        hf = proj_ref[rows, _OFF_HF:_OFF_HF + HG_W]
        hi = proj_ref[rows, _OFF_HI:_OFF_HI + HG_W]
        hg = proj_ref[rows, _OFF_HG:_OFF_HG + HG_W]
        f = lb + (1.0 - lb) * jax.nn.sigmoid(hf)
        p_hi, p_mid, p_lo = _split3_bf16(jnp.log(f))
        b = _dot(tri, p_hi) + _dot(tri, p_mid) + _dot(tri, p_lo)
        b_last = b[CHUNK - 1:CHUNK, :]
        q_dec = (hq * (HG_DIM ** -0.5) * jnp.exp(b)).astype(_BF16)
        k_inv = ((1.0 - f) * jnp.exp(-b)).astype(_BF16)
        k_end = ((1.0 - f) * jnp.exp(b_last - b)).astype(_BF16)
        decay = jnp.exp(b_last)
        gate = jax.nn.silu(hg)
        for hh in range(HG_HEADS):
            sl = slice(hh * HG_DIM, (hh + 1) * HG_DIM)
            a = jnp.where(causal, _dot_nt(q_dec[:, sl], k_inv[:, sl]), 0.0).astype(_BF16)
            v = hi[:, sl]
            st = st_ref[hh]
            o = _dot(a, v.astype(_BF16)) + _dot_nt(q_dec[:, sl], st.astype(_BF16))
            st_ref[hh] = st * decay[:, sl] + _dot(v.T.astype(_BF16), k_end[:, sl])
            y = o * lax.rsqrt(jnp.mean(o * o, axis=-1, keepdims=True) + EPS) * gon * gate[:, sl]
            hy_ref[rows, sl] = y.astype(_BF16)
        return carry

    lax.fori_loop(0, tt // CHUNK, chunk_body, 0)

    ya = _dot(ao_ref[...], wa_ref[...])
    yb = _dot(hy_ref[...], wb_ref[...])
    y = (jax.nn.sigmoid(proj_ref[:, _OFF_GA:_OFF_GA + D_MODEL]) * ya
         + jax.nn.sigmoid(proj_ref[:, _OFF_GB:_OFF_GB + D_MODEL]) * yb)
    o_ref[...] = x_ref[...] + _dot(y.astype(_BF16), wmix_ref[...])

    kbuf_ref[0:KV_HALO, :] = kbuf_ref[tt:tt + KV_HALO, :]
    vbuf_ref[0:KV_HALO, :] = vbuf_ref[tt:tt + KV_HALO, :]


def _const_spec(shape):
    return pl.BlockSpec(shape, lambda b, t: (0,) * len(shape))


def _mixer(x, g_mix, w_in, lb_raw, sinks, g_onorm, w_a, w_b, w_mix, tt):
    bsz, seq, d = x.shape
    tile = pl.BlockSpec((None, tt, d), lambda b, t: (b, t, 0))
    return pl.pallas_call(
        functools.partial(_mixer_kernel, tt=tt),
        grid=(bsz, seq // tt),
        in_specs=[
            tile,
            _const_spec(g_mix.shape),
            _const_spec(w_in.shape),
            _const_spec(lb_raw.shape),
            pl.BlockSpec(memory_space=pltpu.SMEM),
            _const_spec(g_onorm.shape),
            _const_spec(w_a.shape),
            _const_spec(w_b.shape),
            _const_spec(w_mix.shape),
        ],
        out_specs=tile,
        out_shape=jax.ShapeDtypeStruct(x.shape, _F32),
        scratch_shapes=[
            pltpu.VMEM((tt, IN_W), _F32),
            pltpu.VMEM((KV_HALO + tt, ATT_KV_W), _BF16),
            pltpu.VMEM((KV_HALO + tt, ATT_KV_W), _BF16),
            pltpu.VMEM((HG_HEADS, HG_DIM, HG_DIM), _F32),
            pltpu.VMEM((tt, ATT_Q_W), _BF16),
            pltpu.VMEM((tt, HG_W), _BF16),
        ],
        compiler_params=pltpu.CompilerParams(
            dimension_semantics=("arbitrary", "arbitrary"),
            vmem_limit_bytes=VMEM_LIMIT_BYTES),
        name="mixer",
    )(x, g_mix, w_in, lb_raw, sinks, g_onorm, w_a, w_b, w_mix)


def _mem_kv_kernel(mem_ref, gmem_ref, wkv_ref, kv_ref):
    nm = _rms(mem_ref[...], gmem_ref[...]).astype(_BF16)
    kv_ref[...] = _dot(nm, wkv_ref[...]).astype(_BF16)


def _mem_kv(mem, g_mem, w_ckv):
    bsz, m, d = mem.shape
    return pl.pallas_call(
        _mem_kv_kernel,
        grid=(bsz,),
        in_specs=[
            pl.BlockSpec((None, m, d), lambda b: (b, 0, 0)),
            pl.BlockSpec(g_mem.shape, lambda b: (0, 0)),
            pl.BlockSpec(w_ckv.shape, lambda b: (0, 0)),
        ],
        out_specs=pl.BlockSpec((None, m, 2 * d), lambda b: (b, 0, 0)),
        out_shape=jax.ShapeDtypeStruct((bsz, m, 2 * d), _BF16),
        compiler_params=pltpu.CompilerParams(
            dimension_semantics=("arbitrary",), vmem_limit_bytes=VMEM_LIMIT_BYTES),
        name="mem_kv",
    )(mem, g_mem, w_ckv)


def _cross_kernel(h_ref, gc_ref, wq_ref, kv_ref, wo_ref, o_ref):
    h = h_ref[...]
    q = _dot(_rms(h, gc_ref[...]).astype(_BF16), wq_ref[...]).astype(_BF16)
    heads = []
    for hd in range(X_HEADS):
        ks = slice(hd * X_HEAD_DIM, (hd + 1) * X_HEAD_DIM)
        vs = slice(D_MODEL + hd * X_HEAD_DIM, D_MODEL + (hd + 1) * X_HEAD_DIM)
        s = _dot_nt(q[:, ks], kv_ref[:, ks]) * (X_HEAD_DIM ** -0.5)
        p = jnp.exp(s - jnp.max(s, axis=-1, keepdims=True))
        den = jnp.sum(p, axis=-1, keepdims=True)
        heads.append((_dot(p.astype(_BF16), kv_ref[:, vs]) / den).astype(_BF16))
    o_ref[...] = h + _dot(jnp.concatenate(heads, axis=-1), wo_ref[...])


def _cross(h, g_cross, w_cq, kv, w_co, tt):
    bsz, seq, d = h.shape
    m = kv.shape[1]
    tile = pl.BlockSpec((None, tt, d), lambda b, t: (b, t, 0))
    return pl.pallas_call(
        _cross_kernel,
        grid=(bsz, seq // tt),
        in_specs=[
            tile,
            _const_spec(g_cross.shape),
            _const_spec(w_cq.shape),
            pl.BlockSpec((None, m, 2 * d), lambda b, t: (b, 0, 0)),
            _const_spec(w_co.shape),
        ],
        out_specs=tile,
        out_shape=jax.ShapeDtypeStruct(h.shape, _F32),
        compiler_params=pltpu.CompilerParams(
            dimension_semantics=("arbitrary", "arbitrary"),
            vmem_limit_bytes=VMEM_LIMIT_BYTES),
        name="cross",
    )(h, g_cross, w_cq, kv, w_co)


def _ffn_kernel(h_ref, gf_ref, win_ref, cw_ref, cb_ref, wd_ref, gfin_ref, o_ref, ubuf_ref, *, tt):
    t = pl.program_id(1)

    @pl.when(t == 0)
    def _():
        ubuf_ref[0:CONV_HALO, :] = jnp.zeros((CONV_HALO, D_FF), _F32)

    h = h_ref[...]
    n = _rms(h, gf_ref[...]).astype(_BF16)
    ubuf_ref[CONV_HALO:CONV_HALO + tt, :] = _dot(n, win_ref[:, 0:D_FF])
    gate = _dot(n, win_ref[:, D_FF:2 * D_FF])
    u = cb_ref[...]
    for k in range(CONV_WIDTH):
        start = CONV_HALO - (CONV_WIDTH - 1) + k
        u = u + cw_ref[k:k + 1, :] * ubuf_ref[start:start + tt, :]
    act = (jax.nn.silu(u) * gate).astype(_BF16)
    h3 = h + _dot(act, wd_ref[...])
    o_ref[...] = _rms(h3, gfin_ref[...])
    ubuf_ref[0:CONV_HALO, :] = ubuf_ref[tt:tt + CONV_HALO, :]


def _ffn(h, g_ffn, w_ffn_in, conv_w, conv_b, w_down, g_final, tt):
    bsz, seq, d = h.shape
    tile = pl.BlockSpec((None, tt, d), lambda b, t: (b, t, 0))
    return pl.pallas_call(
        functools.partial(_ffn_kernel, tt=tt),
        grid=(bsz, seq // tt),
        in_specs=[
            tile,
            _const_spec(g_ffn.shape),
            _const_spec(w_ffn_in.shape),
            _const_spec(conv_w.shape),
            _const_spec(conv_b.shape),
            _const_spec(w_down.shape),
            _const_spec(g_final.shape),
        ],
        out_specs=tile,
        out_shape=jax.ShapeDtypeStruct(h.shape, _F32),
        scratch_shapes=[pltpu.VMEM((CONV_HALO + tt, D_FF), _F32)],
        compiler_params=pltpu.CompilerParams(
            dimension_semantics=("arbitrary", "arbitrary"),
            vmem_limit_bytes=VMEM_LIMIT_BYTES),
        name="ffn",
    )(h, g_ffn, w_ffn_in, conv_w, conv_b, w_down, g_final)


def kernel(x, mem, g_mix, w_in, lower_bounds, attn_sinks, g_onorm, w_branch_a, w_branch_b,
           w_mix_out, g_cross, g_mem, w_cq, w_ckv, w_co, g_ffn, w_ffn_in, conv_w, conv_b,
           w_ffn_down, g_final):
    depth = g_mix.shape[0]
    assert depth == 1 and x.shape[-1] == D_MODEL and x.shape[1] % TOKEN_TILE == 0
    tt = TOKEN_TILE
    bf = lambda w: w.astype(_BF16)
    h = x
    for l in range(depth):
        h = _mixer(h, g_mix[l][None], bf(w_in[l]), lower_bounds.astype(_F32), attn_sinks[l],
                   g_onorm[l][None], bf(w_branch_a[l]), bf(w_branch_b[l]), bf(w_mix_out[l]), tt)
        kv = _mem_kv(mem, g_mem[l][None], bf(w_ckv[l]))
        h = _cross(h, g_cross[l][None], bf(w_cq[l]), kv, bf(w_co[l]), tt)
        h = _ffn(h, g_ffn[l][None], bf(w_ffn_in[l]), conv_w[l], conv_b[l][None],
                 bf(w_ffn_down[l]), g_final[None], tt)
    return h
```

```python
import functools

import jax
import jax.numpy as jnp
from jax import lax
from jax.experimental import pallas as pl
from jax.experimental.pallas import tpu as pltpu

D_MODEL = 1024
CHUNK = 64
EPS = 1e-6

ATT_HEADS = 8
ATT_KV_HEADS = 2
ATT_HEAD_DIM = 64
ATT_GROUP = ATT_HEADS // ATT_KV_HEADS
LOOKBACK = 2
ATT_Q_W = ATT_HEADS * ATT_HEAD_DIM
ATT_KV_W = ATT_KV_HEADS * ATT_HEAD_DIM
KV_HALO = LOOKBACK * CHUNK
KWIN = (LOOKBACK + 1) * CHUNK

HG_HEADS = 4
HG_DIM = 128
HG_W = HG_HEADS * HG_DIM

X_HEADS = 4
X_HEAD_DIM = D_MODEL // X_HEADS

D_FF = 2816
CONV_WIDTH = 3
CONV_HALO = 8

_OFF_AQ = 0
_OFF_AK = _OFF_AQ + ATT_Q_W
_OFF_AV = _OFF_AK + ATT_KV_W
_OFF_HQ = _OFF_AV + ATT_KV_W
_OFF_HF = _OFF_HQ + HG_W
_OFF_HI = _OFF_HF + HG_W
_OFF_HG = _OFF_HI + HG_W
_OFF_GA = _OFF_HG + HG_W
_OFF_GB = _OFF_GA + D_MODEL
IN_W = _OFF_GB + D_MODEL

TOKEN_TILE = 256
VMEM_LIMIT_BYTES = 56 * 1024 * 1024

_F32 = jnp.float32
_BF16 = jnp.bfloat16
_NT = (((1,), (1,)), ((), ()))


def _rms(xf, gain):
    return xf * lax.rsqrt(jnp.mean(xf * xf, axis=-1, keepdims=True) + EPS) * gain


def _dot(a, b):
    return jnp.dot(a, b, preferred_element_type=_F32)


def _dot_nt(a, b):
    return lax.dot_general(a, b, _NT, preferred_element_type=_F32)


def _split3_bf16(v):
    hi = v.astype(_BF16)
    r1 = v - hi.astype(_F32)
    mid = r1.astype(_BF16)
    lo = (r1 - mid.astype(_F32)).astype(_BF16)
    return hi, mid, lo


def _mixer_kernel(x_ref, gmix_ref, win_ref, lbraw_ref, sinks_ref, gon_ref, wa_ref, wb_ref,
                  wmix_ref, o_ref, n_ref, proj_ref, kbuf_ref, kswp_ref, vbuf_ref, vswp_ref,
                  st_ref, ao_ref, hy_ref, *, tt):
    t = pl.program_id(1)
    nc = tt // CHUNK
    win = KV_HALO + tt

    @pl.when(t == 0)
    def _():
        for ref in (kbuf_ref, kswp_ref, vbuf_ref, vswp_ref):
            ref[0:KV_HALO, :] = jnp.zeros((KV_HALO, ATT_KV_W), _BF16)
        st_ref[...] = jnp.zeros(st_ref.shape, _F32)

    n_ref[...] = _rms(x_ref[...], gmix_ref[...]).astype(_BF16)
    proj_ref[:, 0:_OFF_HQ] = _dot(n_ref[...], win_ref[:, 0:_OFF_HQ])
    proj_ref[:, _OFF_HQ:_OFF_GA] = _dot(n_ref[...], win_ref[:, _OFF_HQ:_OFF_GA])

    new = slice(KV_HALO, win)
    k32 = proj_ref[:, _OFF_AK:_OFF_AK + ATT_KV_W]
    v32 = proj_ref[:, _OFF_AV:_OFF_AV + ATT_KV_W]
    kbuf_ref[new, :] = k32.astype(_BF16)
    vbuf_ref[new, :] = v32.astype(_BF16)
    kswp_ref[new, :] = pltpu.roll(k32, ATT_HEAD_DIM, 1).astype(_BF16)
    vswp_ref[new, :] = pltpu.roll(v32, ATT_HEAD_DIM, 1).astype(_BF16)

    kvlane = lax.broadcasted_iota(jnp.int32, (win, ATT_KV_W), 1)
    halves = (kvlane < ATT_HEAD_DIM, kvlane >= ATT_HEAD_DIM)
    zero_kv = jnp.zeros((win, ATT_KV_W), _BF16)

    def half_placed(nat_ref, swp_ref, g, par):
        src = nat_ref if g == par else swp_ref
        return jnp.where(halves[par], src[...], zero_kv)

    qi = lax.broadcasted_iota(jnp.int32, (tt, win), 0)
    kj = lax.broadcasted_iota(jnp.int32, (tt, win), 1)
    qc = lax.shift_right_logical(qi, 6)
    kc = lax.shift_right_logical(kj, 6)
    allowed = (kc >= qc) & (kc <= qc + LOOKBACK) & ((t * tt + kj) >= KV_HALO)
    absdist = jnp.abs(qi + KV_HALO - kj).astype(_F32)
    q16 = proj_ref[:, _OFF_AQ:_OFF_AQ + ATT_Q_W].astype(_BF16)
    probs, inv_den = [], []
    for h in range(ATT_HEADS):
        g, par, pair = h // ATT_GROUP, h % 2, h // 2
        kz = half_placed(kbuf_ref, kswp_ref, g, par)
        s = _dot_nt(q16[:, 2 * ATT_HEAD_DIM * pair:2 * ATT_HEAD_DIM * (pair + 1)], kz)
        s = jnp.where(allowed, s * (ATT_HEAD_DIM ** -0.5) - (2.0 ** -(h + 1)) * absdist, -jnp.inf)
        sink = sinks_ref[h]
        m = jnp.maximum(jnp.max(s, axis=-1, keepdims=True), sink)
        p = jnp.exp(s - m)
        inv_den.append(1.0 / (jnp.sum(p, axis=-1, keepdims=True) + jnp.exp(sink - m)))
        probs.append(p.astype(_BF16))

    proj_ref[:, _OFF_GA:IN_W] = _dot(n_ref[...], win_ref[:, _OFF_GA:IN_W])

    lraw = lbraw_ref[...]
    lexp = jnp.exp(lraw - jnp.max(lraw, axis=0, keepdims=True))
    lb = lexp[0:1, :] / jnp.sum(lexp, axis=0, keepdims=True)
    ri = lax.broadcasted_iota(jnp.int32, (tt, tt), 0)
    ci = lax.broadcasted_iota(jnp.int32, (tt, tt), 1)
    chunk_causal = (lax.shift_right_logical(ri, 6) == lax.shift_right_logical(ci, 6)) & (ri >= ci)
    tri = chunk_causal.astype(_BF16)
    f = lb + (1.0 - lb) * jax.nn.sigmoid(proj_ref[:, _OFF_HF:_OFF_HF + HG_W])
    p_hi, p_mid, p_lo = _split3_bf16(jnp.log(f))
    b = _dot(tri, p_hi) + _dot(tri, p_mid) + _dot(tri, p_lo)
    b_last = [b[(c + 1) * CHUNK - 1:(c + 1) * CHUNK, :] for c in range(nc)]
    b_last_rows = jnp.concatenate(
        [jnp.broadcast_to(bl, (CHUNK, HG_W)) for bl in b_last], axis=0)
    decay = [jnp.exp(bl) for bl in b_last]
    q_dec = (proj_ref[:, _OFF_HQ:_OFF_HQ + HG_W] * (HG_DIM ** -0.5) * jnp.exp(b)).astype(_BF16)
    k_inv = ((1.0 - f) * jnp.exp(-b)).astype(_BF16)
    k_end = ((1.0 - f) * jnp.exp(b_last_rows - b)).astype(_BF16)

    olane = lax.broadcasted_iota(jnp.int32, (tt, 2 * ATT_HEAD_DIM), 1)
    for pair in range(ATT_HEADS // 2):
        g = (2 * pair) // ATT_GROUP
        o2 = (_dot(probs[2 * pair], half_placed(vbuf_ref, vswp_ref, g, 0))
              + _dot(probs[2 * pair + 1], half_placed(vbuf_ref, vswp_ref, g, 1)))
        scale2 = jnp.where(olane < ATT_HEAD_DIM, inv_den[2 * pair], inv_den[2 * pair + 1])
        ao_ref[:, 2 * ATT_HEAD_DIM * pair:2 * ATT_HEAD_DIM * (pair + 1)] = (o2 * scale2).astype(_BF16)

    rowchunk = lax.shift_right_logical(lax.broadcasted_iota(jnp.int32, (tt, HG_DIM), 0), 6)
    zero_hd = jnp.zeros((tt, HG_DIM), _BF16)

    def chunk_blocks(a):
        return jnp.concatenate([jnp.where(rowchunk == c, a, zero_hd) for c in range(nc)], axis=1)

    a_mats, ds_all, v16 = [], [], []
    for hh in range(HG_HEADS):
        sl = slice(hh * HG_DIM, (hh + 1) * HG_DIM)
        a_mats.append(
            jnp.where(chunk_causal, _dot_nt(q_dec[:, sl], k_inv[:, sl]), 0.0).astype(_BF16))
        v = proj_ref[:, _OFF_HI + hh * HG_DIM:_OFF_HI + (hh + 1) * HG_DIM]
        v16.append(v.astype(_BF16))
        ds_all.append(_dot(v.T.astype(_BF16), chunk_blocks(k_end[:, sl])))

    ya = _dot(ao_ref[...], wa_ref[...])

    gon = gon_ref[...]
    for hh in range(HG_HEADS):
        sl = slice(hh * HG_DIM, (hh + 1) * HG_DIM)
        st = st_ref[hh]
        entering = []
        for c in range(nc):
            entering.append(st.astype(_BF16))
            st = st * decay[c][:, sl] + ds_all[hh][:, c * HG_DIM:(c + 1) * HG_DIM]
        st_ref[hh] = st
        o = (_dot(a_mats[hh], v16[hh])
             + _dot_nt(chunk_blocks(q_dec[:, sl]), jnp.concatenate(entering, axis=1)))
        gate = jax.nn.silu(proj_ref[:, _OFF_HG + hh * HG_DIM:_OFF_HG + (hh + 1) * HG_DIM])
        y = o * lax.rsqrt(jnp.mean(o * o, axis=-1, keepdims=True) + EPS) * gon * gate
        hy_ref[:, sl] = y.astype(_BF16)

    yb = _dot(hy_ref[...], wb_ref[...])
    y = (jax.nn.sigmoid(proj_ref[:, _OFF_GA:_OFF_GA + D_MODEL]) * ya
         + jax.nn.sigmoid(proj_ref[:, _OFF_GB:_OFF_GB + D_MODEL]) * yb)
    o_ref[...] = x_ref[...] + _dot(y.astype(_BF16), wmix_ref[...])

    for ref in (kbuf_ref, kswp_ref, vbuf_ref, vswp_ref):
        ref[0:KV_HALO, :] = ref[tt:tt + KV_HALO, :]


def _const_spec(shape):
    return pl.BlockSpec(shape, lambda b, t: (0,) * len(shape))


def _mixer(x, g_mix, w_in, lb_raw, sinks, g_onorm, w_a, w_b, w_mix, tt):
    bsz, seq, d = x.shape
    tile = pl.BlockSpec((None, tt, d), lambda b, t: (b, t, 0))
    return pl.pallas_call(
        functools.partial(_mixer_kernel, tt=tt),
        grid=(bsz, seq // tt),
        in_specs=[
            tile,
            _const_spec(g_mix.shape),
            _const_spec(w_in.shape),
            _const_spec(lb_raw.shape),
            pl.BlockSpec(memory_space=pltpu.SMEM),
            _const_spec(g_onorm.shape),
            _const_spec(w_a.shape),
            _const_spec(w_b.shape),
            _const_spec(w_mix.shape),
        ],
        out_specs=tile,
        out_shape=jax.ShapeDtypeStruct(x.shape, _F32),
        scratch_shapes=[
            pltpu.VMEM((tt, d), _BF16),
            pltpu.VMEM((tt, IN_W), _F32),
            pltpu.VMEM((KV_HALO + tt, ATT_KV_W), _BF16),
            pltpu.VMEM((KV_HALO + tt, ATT_KV_W), _BF16),
            pltpu.VMEM((KV_HALO + tt, ATT_KV_W), _BF16),
            pltpu.VMEM((KV_HALO + tt, ATT_KV_W), _BF16),
            pltpu.VMEM((HG_HEADS, HG_DIM, HG_DIM), _F32),
            pltpu.VMEM((tt, ATT_Q_W), _BF16),
            pltpu.VMEM((tt, HG_W), _BF16),
        ],
        compiler_params=pltpu.CompilerParams(
            dimension_semantics=("arbitrary", "arbitrary"),
            vmem_limit_bytes=VMEM_LIMIT_BYTES),
        name="mixer",
    )(x, g_mix, w_in, lb_raw, sinks, g_onorm, w_a, w_b, w_mix)


def _mem_kv_kernel(mem_ref, gmem_ref, wkv_ref, kv_ref):
    nm = _rms(mem_ref[...], gmem_ref[...]).astype(_BF16)
    kv_ref[...] = _dot(nm, wkv_ref[...]).astype(_BF16)


def _mem_kv(mem, g_mem, w_ckv):
    bsz, m, d = mem.shape
    return pl.pallas_call(
        _mem_kv_kernel,
        grid=(bsz,),
        in_specs=[
            pl.BlockSpec((None, m, d), lambda b: (b, 0, 0)),
            pl.BlockSpec(g_mem.shape, lambda b: (0, 0)),
            pl.BlockSpec(w_ckv.shape, lambda b: (0, 0)),
        ],
        out_specs=pl.BlockSpec((None, m, 2 * d), lambda b: (b, 0, 0)),
        out_shape=jax.ShapeDtypeStruct((bsz, m, 2 * d), _BF16),
        compiler_params=pltpu.CompilerParams(
            dimension_semantics=("arbitrary",), vmem_limit_bytes=VMEM_LIMIT_BYTES),
        name="mem_kv",
    )(mem, g_mem, w_ckv)


def _cross_kernel(h_ref, gc_ref, wq_ref, kv_ref, wo_ref, o_ref):
    h = h_ref[...]
    q = _dot(_rms(h, gc_ref[...]).astype(_BF16), wq_ref[...]).astype(_BF16)
    heads = []
    for hd in range(X_HEADS):
        ks = slice(hd * X_HEAD_DIM, (hd + 1) * X_HEAD_DIM)
        vs = slice(D_MODEL + hd * X_HEAD_DIM, D_MODEL + (hd + 1) * X_HEAD_DIM)
        s = _dot_nt(q[:, ks], kv_ref[:, ks]) * (X_HEAD_DIM ** -0.5)
        p = jnp.exp(s - jnp.max(s, axis=-1, keepdims=True))
        den = jnp.sum(p, axis=-1, keepdims=True)
        heads.append((_dot(p.astype(_BF16), kv_ref[:, vs]) / den).astype(_BF16))
    o_ref[...] = h + _dot(jnp.concatenate(heads, axis=-1), wo_ref[...])


def _cross(h, g_cross, w_cq, kv, w_co, tt):
    bsz, seq, d = h.shape
    m = kv.shape[1]
    tile = pl.BlockSpec((None, tt, d), lambda b, t: (b, t, 0))
    return pl.pallas_call(
        _cross_kernel,
        grid=(bsz, seq // tt),
        in_specs=[
            tile,
            _const_spec(g_cross.shape),
            _const_spec(w_cq.shape),
            pl.BlockSpec((None, m, 2 * d), lambda b, t: (b, 0, 0)),
            _const_spec(w_co.shape),
        ],
        out_specs=tile,
        out_shape=jax.ShapeDtypeStruct(h.shape, _F32),
        compiler_params=pltpu.CompilerParams(
            dimension_semantics=("arbitrary", "arbitrary"),
            vmem_limit_bytes=VMEM_LIMIT_BYTES),
        name="cross",
    )(h, g_cross, w_cq, kv, w_co)


def _ffn_kernel(h_ref, gf_ref, win_ref, cw_ref, cb_ref, wd_ref, gfin_ref, o_ref, ubuf_ref, *, tt):
    t = pl.program_id(1)

    @pl.when(t == 0)
    def _():
        ubuf_ref[0:CONV_HALO, :] = jnp.zeros((CONV_HALO, D_FF), _F32)

    h = h_ref[...]
    n = _rms(h, gf_ref[...]).astype(_BF16)
    ubuf_ref[CONV_HALO:CONV_HALO + tt, :] = _dot(n, win_ref[:, 0:D_FF])
    gate = _dot(n, win_ref[:, D_FF:2 * D_FF])
    u = cb_ref[...]
    for k in range(CONV_WIDTH):
        start = CONV_HALO - (CONV_WIDTH - 1) + k
        u = u + cw_ref[k:k + 1, :] * ubuf_ref[start:start + tt, :]
    act = (jax.nn.silu(u) * gate).astype(_BF16)
    h3 = h + _dot(act, wd_ref[...])
    o_ref[...] = _rms(h3, gfin_ref[...])
    ubuf_ref[0:CONV_HALO, :] = ubuf_ref[tt:tt + CONV_HALO, :]


def _ffn(h, g_ffn, w_ffn_in, conv_w, conv_b, w_down, g_final, tt):
    bsz, seq, d = h.shape
    tile = pl.BlockSpec((None, tt, d), lambda b, t: (b, t, 0))
    return pl.pallas_call(
        functools.partial(_ffn_kernel, tt=tt),
        grid=(bsz, seq // tt),
        in_specs=[
            tile,
            _const_spec(g_ffn.shape),
            _const_spec(w_ffn_in.shape),
            _const_spec(conv_w.shape),
            _const_spec(conv_b.shape),
            _const_spec(w_down.shape),
            _const_spec(g_final.shape),
        ],
        out_specs=tile,
        out_shape=jax.ShapeDtypeStruct(h.shape, _F32),
        scratch_shapes=[pltpu.VMEM((CONV_HALO + tt, D_FF), _F32)],
        compiler_params=pltpu.CompilerParams(
            dimension_semantics=("arbitrary", "arbitrary"),
            vmem_limit_bytes=VMEM_LIMIT_BYTES),
        name="ffn",
    )(h, g_ffn, w_ffn_in, conv_w, conv_b, w_down, g_final)


def kernel(x, mem, g_mix, w_in, lower_bounds, attn_sinks, g_onorm, w_branch_a, w_branch_b,
           w_mix_out, g_cross, g_mem, w_cq, w_ckv, w_co, g_ffn, w_ffn_in, conv_w, conv_b,
           w_ffn_down, g_final):
    depth = g_mix.shape[0]
    assert depth == 1 and x.shape[-1] == D_MODEL and x.shape[1] % TOKEN_TILE == 0
    tt = TOKEN_TILE
    bf = lambda w: w.astype(_BF16)
    h = x
    for l in range(depth):
        h = _mixer(h, g_mix[l][None], bf(w_in[l]), lower_bounds.astype(_F32), attn_sinks[l],
                   g_onorm[l][None], bf(w_branch_a[l]), bf(w_branch_b[l]), bf(w_mix_out[l]), tt)
        kv = _mem_kv(mem, g_mem[l][None], bf(w_ckv[l]))
        h = _cross(h, g_cross[l][None], bf(w_cq[l]), kv, bf(w_co[l]), tt)
        h = _ffn(h, g_ffn[l][None], bf(w_ffn_in[l]), conv_w[l], conv_b[l][None],
                 bf(w_ffn_down[l]), g_final[None], tt)
    return h
```

```python
import functools

import jax
import jax.numpy as jnp
from jax import lax
from jax.experimental import pallas as pl
from jax.experimental.pallas import tpu as pltpu

D_MODEL = 1024
CHUNK = 64
EPS = 1e-6

ATT_HEADS = 8
ATT_KV_HEADS = 2
ATT_HEAD_DIM = 64
ATT_GROUP = ATT_HEADS // ATT_KV_HEADS
LOOKBACK = 2
ATT_Q_W = ATT_HEADS * ATT_HEAD_DIM
ATT_KV_W = ATT_KV_HEADS * ATT_HEAD_DIM
KV_HALO = LOOKBACK * CHUNK
ATT_QBLK = 2 * CHUNK
ATT_KBLK = ATT_QBLK + KV_HALO

HG_HEADS = 4
HG_DIM = 128
HG_W = HG_HEADS * HG_DIM

X_HEADS = 4
X_HEAD_DIM = D_MODEL // X_HEADS

D_FF = 2816
CONV_WIDTH = 3
SUBLANES = 8

_OFF_AQ = 0
_OFF_AK = _OFF_AQ + ATT_Q_W
_OFF_AV = _OFF_AK + ATT_KV_W
_OFF_HQ = _OFF_AV + ATT_KV_W
_OFF_HF = _OFF_HQ + HG_W
_OFF_HI = _OFF_HF + HG_W
_OFF_HG = _OFF_HI + HG_W
_OFF_GA = _OFF_HG + HG_W
_OFF_GB = _OFF_GA + D_MODEL
IN_W = _OFF_GB + D_MODEL

TOKEN_TILE = 256
CROSS_TILE = 1024
FFN_TILE = 512
VMEM_LIMIT_BYTES = 56 * 1024 * 1024

_F32 = jnp.float32
_BF16 = jnp.bfloat16
_NT = (((1,), (1,)), ((), ()))


def _rms(xf, gain):
    return xf * lax.rsqrt(jnp.mean(xf * xf, axis=-1, keepdims=True) + EPS) * gain


def _dot(a, b):
    return jnp.dot(a, b, preferred_element_type=_F32)


def _dot_nt(a, b):
    return lax.dot_general(a, b, _NT, preferred_element_type=_F32)


def _split3_bf16(v):
    hi = v.astype(_BF16)
    r1 = v - hi.astype(_F32)
    mid = r1.astype(_BF16)
    lo = (r1 - mid.astype(_F32)).astype(_BF16)
    return hi, mid, lo


def _mixer_kernel(x_ref, gmix_ref, win_ref, lbraw_ref, sinks_ref, gon_ref, wa_ref, wb_ref,
                  wmix_ref, o_ref, n_ref, proj_ref, kbuf_ref, kswp_ref, vbuf_ref, vswp_ref,
                  st_ref, ao_ref, hy_ref, *, tt):
    t = pl.program_id(1)
    nc = tt // CHUNK
    win = KV_HALO + tt

    @pl.when(t == 0)
    def _():
        for ref in (kbuf_ref, kswp_ref, vbuf_ref, vswp_ref):
            ref[0:KV_HALO, :] = jnp.zeros((KV_HALO, ATT_KV_W), _BF16)
        st_ref[...] = jnp.zeros(st_ref.shape, _F32)

    n_ref[...] = _rms(x_ref[...], gmix_ref[...]).astype(_BF16)
    proj_ref[:, 0:_OFF_HQ] = _dot(n_ref[...], win_ref[:, 0:_OFF_HQ])
    proj_ref[:, _OFF_HQ:_OFF_GA] = _dot(n_ref[...], win_ref[:, _OFF_HQ:_OFF_GA])

    new = slice(KV_HALO, win)
    k32 = proj_ref[:, _OFF_AK:_OFF_AK + ATT_KV_W]
    v32 = proj_ref[:, _OFF_AV:_OFF_AV + ATT_KV_W]
    kbuf_ref[new, :] = k32.astype(_BF16)
    vbuf_ref[new, :] = v32.astype(_BF16)
    kswp_ref[new, :] = pltpu.roll(k32, ATT_HEAD_DIM, 1).astype(_BF16)
    vswp_ref[new, :] = pltpu.roll(v32, ATT_HEAD_DIM, 1).astype(_BF16)

    kvlane = lax.broadcasted_iota(jnp.int32, (win, ATT_KV_W), 1)
    halves = (kvlane < ATT_HEAD_DIM, kvlane >= ATT_HEAD_DIM)
    zero_kv = jnp.zeros((win, ATT_KV_W), _BF16)

    def half_placed(nat_ref, swp_ref, g, par):
        src = nat_ref if g == par else swp_ref
        return jnp.where(halves[par], src[...], zero_kv)

    kz = [[half_placed(kbuf_ref, kswp_ref, g, par) for par in range(2)]
          for g in range(ATT_KV_HEADS)]
    vz = [[half_placed(vbuf_ref, vswp_ref, g, par) for par in range(2)]
          for g in range(ATT_KV_HEADS)]
    qi = lax.broadcasted_iota(jnp.int32, (ATT_QBLK, ATT_KBLK), 0)
    kj = lax.broadcasted_iota(jnp.int32, (ATT_QBLK, ATT_KBLK), 1)
    qc = lax.shift_right_logical(qi, 6)
    kc = lax.shift_right_logical(kj, 6)
    band = (kc >= qc) & (kc <= qc + LOOKBACK)
    absdist = jnp.abs(qi + KV_HALO - kj).astype(_F32)
    q16 = proj_ref[:, _OFF_AQ:_OFF_AQ + ATT_Q_W].astype(_BF16)
    nblk = tt // ATT_QBLK
    probs = [[None] * ATT_HEADS for _ in range(nblk)]
    inv_den = [[None] * ATT_HEADS for _ in range(nblk)]
    for blk in range(nblk):
        r0 = blk * ATT_QBLK
        allowed = band & ((t * tt + kj) >= KV_HALO) if blk == 0 else band
        for h in range(ATT_HEADS):
            g, par, pair = h // ATT_GROUP, h % 2, h // 2
            qpair = q16[r0:r0 + ATT_QBLK, 2 * ATT_HEAD_DIM * pair:2 * ATT_HEAD_DIM * (pair + 1)]
            s = _dot_nt(qpair, kz[g][par][r0:r0 + ATT_KBLK])
            s = jnp.where(allowed,
                          s * (ATT_HEAD_DIM ** -0.5) - (2.0 ** -(h + 1)) * absdist, -jnp.inf)
            sink = sinks_ref[h]
            m = jnp.maximum(jnp.max(s, axis=-1, keepdims=True), sink)
            p = jnp.exp(s - m)
            inv_den[blk][h] = 1.0 / (jnp.sum(p, axis=-1, keepdims=True) + jnp.exp(sink - m))
            probs[blk][h] = p.astype(_BF16)

    proj_ref[:, _OFF_GA:IN_W] = _dot(n_ref[...], win_ref[:, _OFF_GA:IN_W])

    lraw = lbraw_ref[...]
    lexp = jnp.exp(lraw - jnp.max(lraw, axis=0, keepdims=True))
    lb = lexp[0:1, :] / jnp.sum(lexp, axis=0, keepdims=True)
    ri = lax.broadcasted_iota(jnp.int32, (tt, tt), 0)
    ci = lax.broadcasted_iota(jnp.int32, (tt, tt), 1)
    chunk_causal = (lax.shift_right_logical(ri, 6) == lax.shift_right_logical(ci, 6)) & (ri >= ci)
    tri = chunk_causal.astype(_BF16)
    f = lb + (1.0 - lb) * jax.nn.sigmoid(proj_ref[:, _OFF_HF:_OFF_HF + HG_W])
    p_hi, p_mid, p_lo = _split3_bf16(jnp.log(f))
    b = _dot(tri, p_hi) + _dot(tri, p_mid) + _dot(tri, p_lo)
    b_last = [b[(c + 1) * CHUNK - 1:(c + 1) * CHUNK, :] for c in range(nc)]
    b_last_rows = jnp.concatenate(
        [jnp.broadcast_to(bl, (CHUNK, HG_W)) for bl in b_last], axis=0)
    decay = [jnp.exp(bl) for bl in b_last]
    q_dec = (proj_ref[:, _OFF_HQ:_OFF_HQ + HG_W] * (HG_DIM ** -0.5) * jnp.exp(b)).astype(_BF16)
    k_inv = ((1.0 - f) * jnp.exp(-b)).astype(_BF16)
    k_end = ((1.0 - f) * jnp.exp(b_last_rows - b)).astype(_BF16)

    olane = lax.broadcasted_iota(jnp.int32, (ATT_QBLK, 2 * ATT_HEAD_DIM), 1)
    for blk in range(nblk):
        r0 = blk * ATT_QBLK
        for pair in range(ATT_HEADS // 2):
            g, he, ho = (2 * pair) // ATT_GROUP, 2 * pair, 2 * pair + 1
            o2 = (_dot(probs[blk][he], vz[g][0][r0:r0 + ATT_KBLK])
                  + _dot(probs[blk][ho], vz[g][1][r0:r0 + ATT_KBLK]))
            scale2 = jnp.where(olane < ATT_HEAD_DIM, inv_den[blk][he], inv_den[blk][ho])
            ao_ref[r0:r0 + ATT_QBLK, 2 * ATT_HEAD_DIM * pair:2 * ATT_HEAD_DIM * (pair + 1)] = (
                (o2 * scale2).astype(_BF16))

    rowchunk = lax.shift_right_logical(lax.broadcasted_iota(jnp.int32, (tt, HG_DIM), 0), 6)
    zero_hd = jnp.zeros((tt, HG_DIM), _BF16)

    def chunk_blocks(a):
        return jnp.concatenate([jnp.where(rowchunk == c, a, zero_hd) for c in range(nc)], axis=1)

    a_mats, ds_all, v16 = [], [], []
    for hh in range(HG_HEADS):
        sl = slice(hh * HG_DIM, (hh + 1) * HG_DIM)
        a_mats.append(
            jnp.where(chunk_causal, _dot_nt(q_dec[:, sl], k_inv[:, sl]), 0.0).astype(_BF16))
        v = proj_ref[:, _OFF_HI + hh * HG_DIM:_OFF_HI + (hh + 1) * HG_DIM]
        v16.append(v.astype(_BF16))
        ds_all.append(_dot(v.T.astype(_BF16), chunk_blocks(k_end[:, sl])))

    ya = _dot(ao_ref[...], wa_ref[...])

    gon = gon_ref[...]
    for hh in range(HG_HEADS):
        sl = slice(hh * HG_DIM, (hh + 1) * HG_DIM)
        st = st_ref[hh]
        entering = []
        for c in range(nc):
            entering.append(st.astype(_BF16))
            st = st * decay[c][:, sl] + ds_all[hh][:, c * HG_DIM:(c + 1) * HG_DIM]
        st_ref[hh] = st
        o = (_dot(a_mats[hh], v16[hh])
             + _dot_nt(chunk_blocks(q_dec[:, sl]), jnp.concatenate(entering, axis=1)))
        gate = jax.nn.silu(proj_ref[:, _OFF_HG + hh * HG_DIM:_OFF_HG + (hh + 1) * HG_DIM])
        y = o * lax.rsqrt(jnp.mean(o * o, axis=-1, keepdims=True) + EPS) * gon * gate
        hy_ref[:, sl] = y.astype(_BF16)

    yb = _dot(hy_ref[...], wb_ref[...])
    y = (jax.nn.sigmoid(proj_ref[:, _OFF_GA:_OFF_GA + D_MODEL]) * ya
         + jax.nn.sigmoid(proj_ref[:, _OFF_GB:_OFF_GB + D_MODEL]) * yb)
    o_ref[...] = x_ref[...] + _dot(y.astype(_BF16), wmix_ref[...])

    for ref in (kbuf_ref, kswp_ref, vbuf_ref, vswp_ref):
        ref[0:KV_HALO, :] = ref[tt:tt + KV_HALO, :]


def _const_spec(shape):
    return pl.BlockSpec(shape, lambda b, t: (0,) * len(shape), pipeline_mode=pl.Buffered(1))


def _mixer(x, g_mix, w_in, lb_raw, sinks, g_onorm, w_a, w_b, w_mix, tt):
    bsz, seq, d = x.shape
    tile = pl.BlockSpec((None, tt, d), lambda b, t: (b, t, 0))
    return pl.pallas_call(
        functools.partial(_mixer_kernel, tt=tt),
        grid=(bsz, seq // tt),
        in_specs=[
            tile,
            _const_spec(g_mix.shape),
            _const_spec(w_in.shape),
            _const_spec(lb_raw.shape),
            pl.BlockSpec(memory_space=pltpu.SMEM),
            _const_spec(g_onorm.shape),
            _const_spec(w_a.shape),
            _const_spec(w_b.shape),
            _const_spec(w_mix.shape),
        ],
        out_specs=tile,
        out_shape=jax.ShapeDtypeStruct(x.shape, _F32),
        scratch_shapes=[
            pltpu.VMEM((tt, d), _BF16),
            pltpu.VMEM((tt, IN_W), _F32),
            pltpu.VMEM((KV_HALO + tt, ATT_KV_W), _BF16),
            pltpu.VMEM((KV_HALO + tt, ATT_KV_W), _BF16),
            pltpu.VMEM((KV_HALO + tt, ATT_KV_W), _BF16),
            pltpu.VMEM((KV_HALO + tt, ATT_KV_W), _BF16),
            pltpu.VMEM((HG_HEADS, HG_DIM, HG_DIM), _F32),
            pltpu.VMEM((tt, ATT_Q_W), _BF16),
            pltpu.VMEM((tt, HG_W), _BF16),
        ],
        compiler_params=pltpu.CompilerParams(
            dimension_semantics=("arbitrary", "arbitrary"),
            vmem_limit_bytes=VMEM_LIMIT_BYTES),
        name="mixer",
    )(x, g_mix, w_in, lb_raw, sinks, g_onorm, w_a, w_b, w_mix)


def _mem_kv_kernel(mem_ref, gmem_ref, wkv_ref, kv_ref):
    nm = _rms(mem_ref[...], gmem_ref[...]).astype(_BF16)
    kv_ref[...] = _dot(nm, wkv_ref[...]).astype(_BF16)


def _mem_kv(mem, g_mem, w_ckv):
    bsz, m, d = mem.shape
    return pl.pallas_call(
        _mem_kv_kernel,
        grid=(bsz,),
        in_specs=[
            pl.BlockSpec((None, m, d), lambda b: (b, 0, 0)),
            pl.BlockSpec(g_mem.shape, lambda b: (0, 0)),
            pl.BlockSpec(w_ckv.shape, lambda b: (0, 0)),
        ],
        out_specs=pl.BlockSpec((None, m, 2 * d), lambda b: (b, 0, 0)),
        out_shape=jax.ShapeDtypeStruct((bsz, m, 2 * d), _BF16),
        compiler_params=pltpu.CompilerParams(
            dimension_semantics=("arbitrary",), vmem_limit_bytes=VMEM_LIMIT_BYTES),
        name="mem_kv",
    )(mem, g_mem, w_ckv)


def _cross_kernel(h_ref, gc_ref, wq_ref, kv_ref, wo_ref, o_ref):
    h = h_ref[...]
    q = _dot(_rms(h, gc_ref[...]).astype(_BF16), wq_ref[...]).astype(_BF16)
    heads = []
    for hd in range(X_HEADS):
        ks = slice(hd * X_HEAD_DIM, (hd + 1) * X_HEAD_DIM)
        vs = slice(D_MODEL + hd * X_HEAD_DIM, D_MODEL + (hd + 1) * X_HEAD_DIM)
        s = _dot_nt(q[:, ks], kv_ref[:, ks]) * (X_HEAD_DIM ** -0.5)
        p = jnp.exp(s - jnp.max(s, axis=-1, keepdims=True))
        den = jnp.sum(p, axis=-1, keepdims=True)
        heads.append((_dot(p.astype(_BF16), kv_ref[:, vs]) / den).astype(_BF16))
    o_ref[...] = h + _dot(jnp.concatenate(heads, axis=-1), wo_ref[...])


def _cross(h, g_cross, w_cq, kv, w_co, tt):
    bsz, seq, d = h.shape
    m = kv.shape[1]
    tile = pl.BlockSpec((None, tt, d), lambda b, t: (b, t, 0))
    return pl.pallas_call(
        _cross_kernel,
        grid=(bsz, seq // tt),
        in_specs=[
            tile,
            _const_spec(g_cross.shape),
            _const_spec(w_cq.shape),
            pl.BlockSpec((None, m, 2 * d), lambda b, t: (b, 0, 0)),
            _const_spec(w_co.shape),
        ],
        out_specs=tile,
        out_shape=jax.ShapeDtypeStruct(h.shape, _F32),
        compiler_params=pltpu.CompilerParams(
            dimension_semantics=("arbitrary", "arbitrary"),
            vmem_limit_bytes=VMEM_LIMIT_BYTES),
        name="cross",
    )(h, g_cross, w_cq, kv, w_co)


def _shift_rows(a, first_row):
    rolled = pltpu.roll(a, 1, 0)
    row = lax.broadcasted_iota(jnp.int32, (SUBLANES, a.shape[1]), 0)
    head = jnp.where(row == 0, first_row, rolled[0:SUBLANES])
    return jnp.concatenate([head, rolled[SUBLANES:]], axis=0)


def _ffn_kernel(h_ref, gf_ref, win_ref, cw_ref, cb_ref, wd_ref, gfin_ref, o_ref, tail_ref, *, tt):
    t = pl.program_id(1)

    @pl.when(t == 0)
    def _():
        tail_ref[...] = jnp.zeros(tail_ref.shape, _F32)

    h = h_ref[...]
    n = _rms(h, gf_ref[...]).astype(_BF16)
    u = _dot(n, win_ref[:, 0:D_FF])
    gate = _dot(n, win_ref[:, D_FF:2 * D_FF])
    w0, w1, w2 = cw_ref[0:1, :], cw_ref[1:2, :], cw_ref[2:3, :]
    prev1 = tail_ref[SUBLANES - 1:SUBLANES, :]
    prev2 = tail_ref[SUBLANES - 2:SUBLANES - 1, :]
    acc = w1 * u + _shift_rows(w0 * u, w0 * prev1)
    conv = w2 * u + _shift_rows(acc, w1 * prev1 + w0 * prev2) + cb_ref[...]
    tail_ref[...] = u[tt - SUBLANES:tt, :]
    act = (jax.nn.silu(conv) * gate).astype(_BF16)
    h3 = h + _dot(act, wd_ref[...])
    o_ref[...] = _rms(h3, gfin_ref[...])


def _ffn(h, g_ffn, w_ffn_in, conv_w, conv_b, w_down, g_final, tt):
    bsz, seq, d = h.shape
    tile = pl.BlockSpec((None, tt, d), lambda b, t: (b, t, 0))
    return pl.pallas_call(
        functools.partial(_ffn_kernel, tt=tt),
        grid=(bsz, seq // tt),
        in_specs=[
            tile,
            _const_spec(g_ffn.shape),
            _const_spec(w_ffn_in.shape),
            _const_spec(conv_w.shape),
            _const_spec(conv_b.shape),
            _const_spec(w_down.shape),
            _const_spec(g_final.shape),
        ],
        out_specs=tile,
        out_shape=jax.ShapeDtypeStruct(h.shape, _F32),
        scratch_shapes=[pltpu.VMEM((SUBLANES, D_FF), _F32)],
        compiler_params=pltpu.CompilerParams(
            dimension_semantics=("arbitrary", "arbitrary"),
            vmem_limit_bytes=VMEM_LIMIT_BYTES),
        name="ffn",
    )(h, g_ffn, w_ffn_in, conv_w, conv_b, w_down, g_final)


def kernel(x, mem, g_mix, w_in, lower_bounds, attn_sinks, g_onorm, w_branch_a, w_branch_b,
           w_mix_out, g_cross, g_mem, w_cq, w_ckv, w_co, g_ffn, w_ffn_in, conv_w, conv_b,
           w_ffn_down, g_final):
    depth = g_mix.shape[0]
    assert depth == 1 and x.shape[-1] == D_MODEL and x.shape[1] % TOKEN_TILE == 0
    tt = TOKEN_TILE
    bf = lambda w: w.astype(_BF16)
    h = x
    for l in range(depth):
        h = _mixer(h, g_mix[l][None], bf(w_in[l]), lower_bounds.astype(_F32), attn_sinks[l],
                   g_onorm[l][None], bf(w_branch_a[l]), bf(w_branch_b[l]), bf(w_mix_out[l]), tt)
        kv = _mem_kv(mem, g_mem[l][None], bf(w_ckv[l]))
        h = _cross(h, g_cross[l][None], bf(w_cq[l]), kv, bf(w_co[l]), CROSS_TILE)
        h = _ffn(h, g_ffn[l][None], bf(w_ffn_in[l]), conv_w[l], conv_b[l][None],
                 bf(w_ffn_down[l]), g_final[None], FFN_TILE)
    return h
```

```python
import functools

import jax
import jax.numpy as jnp
from jax import lax
from jax.experimental import pallas as pl
from jax.experimental.pallas import tpu as pltpu

D_MODEL = 1024
CHUNK = 64
EPS = 1e-6

ATT_HEADS = 8
ATT_KV_HEADS = 2
ATT_HEAD_DIM = 64
ATT_GROUP = ATT_HEADS // ATT_KV_HEADS
LOOKBACK = 2
ATT_Q_W = ATT_HEADS * ATT_HEAD_DIM
ATT_KV_W = ATT_KV_HEADS * ATT_HEAD_DIM
KV_HALO = LOOKBACK * CHUNK
ATT_QBLK = 2 * CHUNK
ATT_KBLK = ATT_QBLK + KV_HALO

HG_HEADS = 4
HG_DIM = 128
HG_W = HG_HEADS * HG_DIM

X_HEADS = 4
X_HEAD_DIM = D_MODEL // X_HEADS

D_FF = 2816
CONV_WIDTH = 3
SUBLANES = 8
BF16_ROWS = 16

_OFF_AQ = 0
_OFF_AK = _OFF_AQ + ATT_Q_W
_OFF_AV = _OFF_AK + ATT_KV_W
_OFF_HQ = _OFF_AV + ATT_KV_W
_OFF_HF = _OFF_HQ + HG_W
_OFF_HI = _OFF_HF + HG_W
_OFF_HG = _OFF_HI + HG_W
_OFF_GA = _OFF_HG + HG_W
_OFF_GB = _OFF_GA + D_MODEL
IN_W = _OFF_GB + D_MODEL

TOKEN_TILE = 256
CROSS_TILE = 1024
FFN_TILE = 512
VMEM_LIMIT_BYTES = 56 * 1024 * 1024

_F32 = jnp.float32
_BF16 = jnp.bfloat16
_NT = (((1,), (1,)), ((), ()))


def _rms(xf, gain):
    return xf * lax.rsqrt(jnp.mean(xf * xf, axis=-1, keepdims=True) + EPS) * gain


def _dot(a, b):
    return jnp.dot(a, b, preferred_element_type=_F32)


def _dot_nt(a, b):
    return lax.dot_general(a, b, _NT, preferred_element_type=_F32)


def _chunk_cumsum(a):
    rows, cols = a.shape
    sub = lax.broadcasted_iota(jnp.int32, (SUBLANES, cols), 0)
    groups = []
    for g0 in range(0, rows, SUBLANES):
        g = a[g0:g0 + SUBLANES, :]
        step = 1
        while step < SUBLANES:
            g = g + jnp.where(sub >= step, pltpu.roll(g, step, 0), 0.0)
            step *= 2
        if g0 % CHUNK:
            g = g + carry
        carry = jnp.broadcast_to(g[SUBLANES - 1:SUBLANES, :], (SUBLANES, cols))
        groups.append(g)
    return jnp.concatenate(groups, axis=0)


def _mixer_kernel(x_ref, gmix_ref, win_ref, lbraw_ref, sinks_ref, gon_ref, wa_ref, wb_ref,
                  wmix_ref, *rest, tt, n_later):
    later_f32 = rest[:n_later]
    o_ref = rest[n_later]
    later_bf16 = rest[n_later + 1:2 * n_later + 1]
    (n_ref, proj_ref, kbuf_ref, kswp_ref, vbuf_ref, vswp_ref,
     st_ref, ao_ref, hy_ref) = rest[2 * n_later + 1:]
    t = pl.program_id(1)
    nc = tt // CHUNK
    win = KV_HALO + tt

    @pl.when(t == 0)
    def _():
        for ref in (kbuf_ref, kswp_ref, vbuf_ref, vswp_ref):
            ref[0:KV_HALO, :] = jnp.zeros((KV_HALO, ATT_KV_W), _BF16)
        st_ref[...] = jnp.zeros(st_ref.shape, _F32)

    n_ref[...] = _rms(x_ref[...], gmix_ref[...]).astype(_BF16)
    proj_ref[:, 0:_OFF_HQ] = _dot(n_ref[...], win_ref[:, 0:_OFF_HQ])
    proj_ref[:, _OFF_HQ:_OFF_GA] = _dot(n_ref[...], win_ref[:, _OFF_HQ:_OFF_GA])

    new = slice(KV_HALO, win)
    k32 = proj_ref[:, _OFF_AK:_OFF_AK + ATT_KV_W]
    v32 = proj_ref[:, _OFF_AV:_OFF_AV + ATT_KV_W]
    kbuf_ref[new, :] = k32.astype(_BF16)
    vbuf_ref[new, :] = v32.astype(_BF16)
    kswp_ref[new, :] = pltpu.roll(k32, ATT_HEAD_DIM, 1).astype(_BF16)
    vswp_ref[new, :] = pltpu.roll(v32, ATT_HEAD_DIM, 1).astype(_BF16)

    kvlane = lax.broadcasted_iota(jnp.int32, (win, ATT_KV_W), 1)
    halves = (kvlane < ATT_HEAD_DIM, kvlane >= ATT_HEAD_DIM)
    zero_kv = jnp.zeros((win, ATT_KV_W), _BF16)

    def half_placed(nat_ref, swp_ref, g, par):
        src = nat_ref if g == par else swp_ref
        return jnp.where(halves[par], src[...], zero_kv)

    kz = [[half_placed(kbuf_ref, kswp_ref, g, par) for par in range(2)]
          for g in range(ATT_KV_HEADS)]
    vz = [[half_placed(vbuf_ref, vswp_ref, g, par) for par in range(2)]
          for g in range(ATT_KV_HEADS)]
    qi = lax.broadcasted_iota(jnp.int32, (ATT_QBLK, ATT_KBLK), 0)
    kj = lax.broadcasted_iota(jnp.int32, (ATT_QBLK, ATT_KBLK), 1)
    qc = lax.shift_right_logical(qi, 6)
    kc = lax.shift_right_logical(kj, 6)
    band = (kc >= qc) & (kc <= qc + LOOKBACK)
    absdist = jnp.abs(qi + KV_HALO - kj).astype(_F32)
    q16 = proj_ref[:, _OFF_AQ:_OFF_AQ + ATT_Q_W].astype(_BF16)
    nblk = tt // ATT_QBLK
    probs = [[None] * ATT_HEADS for _ in range(nblk)]
    inv_den = [[None] * ATT_HEADS for _ in range(nblk)]
    for blk in range(nblk):
        r0 = blk * ATT_QBLK
        allowed = band & ((t * tt + kj) >= KV_HALO) if blk == 0 else band
        for h in range(ATT_HEADS):
            g, par, pair = h // ATT_GROUP, h % 2, h // 2
            qpair = q16[r0:r0 + ATT_QBLK, 2 * ATT_HEAD_DIM * pair:2 * ATT_HEAD_DIM * (pair + 1)]
            s = _dot_nt(qpair, kz[g][par][r0:r0 + ATT_KBLK])
            s = jnp.where(allowed,
                          s * (ATT_HEAD_DIM ** -0.5) - (2.0 ** -(h + 1)) * absdist, -jnp.inf)
            sink = sinks_ref[h]
            m = jnp.maximum(jnp.max(s, axis=-1, keepdims=True), sink)
            p = jnp.exp(s - m)
            inv_den[blk][h] = 1.0 / (jnp.sum(p, axis=-1, keepdims=True) + jnp.exp(sink - m))
            probs[blk][h] = p.astype(_BF16)

    proj_ref[:, _OFF_GA:IN_W] = _dot(n_ref[...], win_ref[:, _OFF_GA:IN_W])

    lraw = lbraw_ref[...]
    lexp = jnp.exp(lraw - jnp.max(lraw, axis=0, keepdims=True))
    lb = lexp[0:1, :] / jnp.sum(lexp, axis=0, keepdims=True)
    ri = lax.broadcasted_iota(jnp.int32, (tt, tt), 0)
    ci = lax.broadcasted_iota(jnp.int32, (tt, tt), 1)
    chunk_causal = (lax.shift_right_logical(ri, 6) == lax.shift_right_logical(ci, 6)) & (ri >= ci)
    f = lb + (1.0 - lb) * jax.nn.sigmoid(proj_ref[:, _OFF_HF:_OFF_HF + HG_W])
    b = _chunk_cumsum(jnp.log(f))
    b_last = [b[(c + 1) * CHUNK - 1:(c + 1) * CHUNK, :] for c in range(nc)]
    b_last_rows = jnp.concatenate(
        [jnp.broadcast_to(bl, (CHUNK, HG_W)) for bl in b_last], axis=0)
    decay = [jnp.exp(bl) for bl in b_last]
    q_dec = (proj_ref[:, _OFF_HQ:_OFF_HQ + HG_W] * (HG_DIM ** -0.5) * jnp.exp(b)).astype(_BF16)
    k_inv = ((1.0 - f) * jnp.exp(-b)).astype(_BF16)
    k_end = ((1.0 - f) * jnp.exp(b_last_rows - b)).astype(_BF16)

    olane = lax.broadcasted_iota(jnp.int32, (ATT_QBLK, 2 * ATT_HEAD_DIM), 1)
    for blk in range(nblk):
        r0 = blk * ATT_QBLK
        for pair in range(ATT_HEADS // 2):
            g, he, ho = (2 * pair) // ATT_GROUP, 2 * pair, 2 * pair + 1
            o2 = (_dot(probs[blk][he], vz[g][0][r0:r0 + ATT_KBLK])
                  + _dot(probs[blk][ho], vz[g][1][r0:r0 + ATT_KBLK]))
            scale2 = jnp.where(olane < ATT_HEAD_DIM, inv_den[blk][he], inv_den[blk][ho])
            ao_ref[r0:r0 + ATT_QBLK, 2 * ATT_HEAD_DIM * pair:2 * ATT_HEAD_DIM * (pair + 1)] = (
                (o2 * scale2).astype(_BF16))

    rowchunk = lax.shift_right_logical(lax.broadcasted_iota(jnp.int32, (tt, HG_DIM), 0), 6)
    zero_hd = jnp.zeros((tt, HG_DIM), _BF16)

    def chunk_blocks(a):
        return jnp.concatenate([jnp.where(rowchunk == c, a, zero_hd) for c in range(nc)], axis=1)

    a_mats, ds_all, v16 = [], [], []
    for hh in range(HG_HEADS):
        sl = slice(hh * HG_DIM, (hh + 1) * HG_DIM)
        a_mats.append(
            jnp.where(chunk_causal, _dot_nt(q_dec[:, sl], k_inv[:, sl]), 0.0).astype(_BF16))
        v = proj_ref[:, _OFF_HI + hh * HG_DIM:_OFF_HI + (hh + 1) * HG_DIM]
        v16.append(v.astype(_BF16))
        ds_all.append(_dot(v.T.astype(_BF16), chunk_blocks(k_end[:, sl])))

    ya = _dot(ao_ref[...], wa_ref[...])

    gon = gon_ref[...]
    for hh in range(HG_HEADS):
        sl = slice(hh * HG_DIM, (hh + 1) * HG_DIM)
        st = st_ref[hh]
        entering = []
        for c in range(nc):
            entering.append(st.astype(_BF16))
            st = st * decay[c][:, sl] + ds_all[hh][:, c * HG_DIM:(c + 1) * HG_DIM]
        st_ref[hh] = st
        o = (_dot(a_mats[hh], v16[hh])
             + _dot_nt(chunk_blocks(q_dec[:, sl]), jnp.concatenate(entering, axis=1)))
        gate = jax.nn.silu(proj_ref[:, _OFF_HG + hh * HG_DIM:_OFF_HG + (hh + 1) * HG_DIM])
        y = o * lax.rsqrt(jnp.mean(o * o, axis=-1, keepdims=True) + EPS) * gon * gate
        hy_ref[:, sl] = y.astype(_BF16)

    yb = _dot(hy_ref[...], wb_ref[...])
    y = (jax.nn.sigmoid(proj_ref[:, _OFF_GA:_OFF_GA + D_MODEL]) * ya
         + jax.nn.sigmoid(proj_ref[:, _OFF_GB:_OFF_GB + D_MODEL]) * yb)
    o_ref[...] = x_ref[...] + _dot(y.astype(_BF16), wmix_ref[...])

    for ref in (kbuf_ref, kswp_ref, vbuf_ref, vswp_ref):
        ref[0:KV_HALO, :] = ref[tt:tt + KV_HALO, :]
    for src, dst in zip(later_f32, later_bf16):
        dst[...] = src[...].astype(_BF16)


def _const_spec(shape):
    return pl.BlockSpec(shape, lambda b, t: (0,) * len(shape), pipeline_mode=pl.Buffered(1))


def _slab_spec(shape, nt, nsteps):
    rows, cols = shape
    slab = next(s for s in range(BF16_ROWS, rows + 1, BF16_ROWS)
                if rows % s == 0 and rows // s <= nsteps)
    nslabs = rows // slab
    return pl.BlockSpec((slab, cols), lambda b, t: ((b * nt + t) * nslabs // nsteps, 0))


def _mixer(x, g_mix, w_in, lb_raw, sinks, g_onorm, w_a, w_b, w_mix, later_weights, tt):
    bsz, seq, d = x.shape
    nt = seq // tt
    tile = pl.BlockSpec((None, tt, d), lambda b, t: (b, t, 0))
    slabs = [_slab_spec(w.shape, nt, bsz * nt) for w in later_weights]
    h, *later16 = pl.pallas_call(
        functools.partial(_mixer_kernel, tt=tt, n_later=len(later_weights)),
        grid=(bsz, nt),
        in_specs=[
            tile,
            _const_spec(g_mix.shape),
            _const_spec(w_in.shape),
            _const_spec(lb_raw.shape),
            pl.BlockSpec(memory_space=pltpu.SMEM),
            _const_spec(g_onorm.shape),
            _const_spec(w_a.shape),
            _const_spec(w_b.shape),
            _const_spec(w_mix.shape),
        ] + slabs,
        out_specs=[tile] + slabs,
        out_shape=[jax.ShapeDtypeStruct(x.shape, _F32)]
        + [jax.ShapeDtypeStruct(w.shape, _BF16) for w in later_weights],
        scratch_shapes=[
            pltpu.VMEM((tt, d), _BF16),
            pltpu.VMEM((tt, IN_W), _F32),
            pltpu.VMEM((KV_HALO + tt, ATT_KV_W), _BF16),
            pltpu.VMEM((KV_HALO + tt, ATT_KV_W), _BF16),
            pltpu.VMEM((KV_HALO + tt, ATT_KV_W), _BF16),
            pltpu.VMEM((KV_HALO + tt, ATT_KV_W), _BF16),
            pltpu.VMEM((HG_HEADS, HG_DIM, HG_DIM), _F32),
            pltpu.VMEM((tt, ATT_Q_W), _BF16),
            pltpu.VMEM((tt, HG_W), _BF16),
        ],
        compiler_params=pltpu.CompilerParams(
            dimension_semantics=("arbitrary", "arbitrary"),
            vmem_limit_bytes=VMEM_LIMIT_BYTES),
        name="mixer",
    )(x, g_mix, w_in, lb_raw, sinks, g_onorm, w_a, w_b, w_mix, *later_weights)
    return h, later16


def _mem_kv_kernel(mem_ref, gmem_ref, wkv_ref, kv_ref):
    nm = _rms(mem_ref[...], gmem_ref[...]).astype(_BF16)
    kv_ref[...] = _dot(nm, wkv_ref[...]).astype(_BF16)


def _mem_kv(mem, g_mem, w_ckv):
    bsz, m, d = mem.shape
    return pl.pallas_call(
        _mem_kv_kernel,
        grid=(bsz,),
        in_specs=[
            pl.BlockSpec((None, m, d), lambda b: (b, 0, 0)),
            pl.BlockSpec(g_mem.shape, lambda b: (0, 0)),
            pl.BlockSpec(w_ckv.shape, lambda b: (0, 0)),
        ],
        out_specs=pl.BlockSpec((None, m, 2 * d), lambda b: (b, 0, 0)),
        out_shape=jax.ShapeDtypeStruct((bsz, m, 2 * d), _BF16),
        compiler_params=pltpu.CompilerParams(
            dimension_semantics=("arbitrary",), vmem_limit_bytes=VMEM_LIMIT_BYTES),
        name="mem_kv",
    )(mem, g_mem, w_ckv)


def _cross_kernel(h_ref, gc_ref, wq_ref, kv_ref, wo_ref, o_ref):
    h = h_ref[...]
    q = _dot(_rms(h, gc_ref[...]).astype(_BF16), wq_ref[...]).astype(_BF16)
    heads = []
    for hd in range(X_HEADS):
        ks = slice(hd * X_HEAD_DIM, (hd + 1) * X_HEAD_DIM)
        vs = slice(D_MODEL + hd * X_HEAD_DIM, D_MODEL + (hd + 1) * X_HEAD_DIM)
        s = _dot_nt(q[:, ks], kv_ref[:, ks]) * (X_HEAD_DIM ** -0.5)
        p = jnp.exp(s - jnp.max(s, axis=-1, keepdims=True))
        den = jnp.sum(p, axis=-1, keepdims=True)
        heads.append((_dot(p.astype(_BF16), kv_ref[:, vs]) / den).astype(_BF16))
    o_ref[...] = h + _dot(jnp.concatenate(heads, axis=-1), wo_ref[...])


def _cross(h, g_cross, w_cq, kv, w_co, tt):
    bsz, seq, d = h.shape
    m = kv.shape[1]
    tile = pl.BlockSpec((None, tt, d), lambda b, t: (b, t, 0))
    return pl.pallas_call(
        _cross_kernel,
        grid=(bsz, seq // tt),
        in_specs=[
            tile,
            _const_spec(g_cross.shape),
            _const_spec(w_cq.shape),
            pl.BlockSpec((None, m, 2 * d), lambda b, t: (b, 0, 0)),
            _const_spec(w_co.shape),
        ],
        out_specs=tile,
        out_shape=jax.ShapeDtypeStruct(h.shape, _F32),
        compiler_params=pltpu.CompilerParams(
            dimension_semantics=("arbitrary", "arbitrary"),
            vmem_limit_bytes=VMEM_LIMIT_BYTES),
        name="cross",
    )(h, g_cross, w_cq, kv, w_co)


def _shift_rows(a, first_row):
    rolled = pltpu.roll(a, 1, 0)
    row = lax.broadcasted_iota(jnp.int32, (SUBLANES, a.shape[1]), 0)
    head = jnp.where(row == 0, first_row, rolled[0:SUBLANES])
    return jnp.concatenate([head, rolled[SUBLANES:]], axis=0)


def _ffn_kernel(h_ref, gf_ref, win_ref, cw_ref, cb_ref, wd_ref, gfin_ref, o_ref, tail_ref, *, tt):
    t = pl.program_id(1)

    @pl.when(t == 0)
    def _():
        tail_ref[...] = jnp.zeros(tail_ref.shape, _F32)

    h = h_ref[...]
    n = _rms(h, gf_ref[...]).astype(_BF16)
    u = _dot(n, win_ref[:, 0:D_FF])
    gate = _dot(n, win_ref[:, D_FF:2 * D_FF])
    w0, w1, w2 = cw_ref[0:1, :], cw_ref[1:2, :], cw_ref[2:3, :]
    prev1 = tail_ref[SUBLANES - 1:SUBLANES, :]
    prev2 = tail_ref[SUBLANES - 2:SUBLANES - 1, :]
    acc = w1 * u + _shift_rows(w0 * u, w0 * prev1)
    conv = w2 * u + _shift_rows(acc, w1 * prev1 + w0 * prev2) + cb_ref[...]
    tail_ref[...] = u[tt - SUBLANES:tt, :]
    act = (jax.nn.silu(conv) * gate).astype(_BF16)
    h3 = h + _dot(act, wd_ref[...])
    o_ref[...] = _rms(h3, gfin_ref[...])


def _ffn(h, g_ffn, w_ffn_in, conv_w, conv_b, w_down, g_final, tt):
    bsz, seq, d = h.shape
    tile = pl.BlockSpec((None, tt, d), lambda b, t: (b, t, 0))
    return pl.pallas_call(
        functools.partial(_ffn_kernel, tt=tt),
        grid=(bsz, seq // tt),
        in_specs=[
            tile,
            _const_spec(g_ffn.shape),
            _const_spec(w_ffn_in.shape),
            _const_spec(conv_w.shape),
            _const_spec(conv_b.shape),
            _const_spec(w_down.shape),
            _const_spec(g_final.shape),
        ],
        out_specs=tile,
        out_shape=jax.ShapeDtypeStruct(h.shape, _F32),
        scratch_shapes=[pltpu.VMEM((SUBLANES, D_FF), _F32)],
        compiler_params=pltpu.CompilerParams(
            dimension_semantics=("arbitrary", "arbitrary"),
            vmem_limit_bytes=VMEM_LIMIT_BYTES),
        name="ffn",
    )(h, g_ffn, w_ffn_in, conv_w, conv_b, w_down, g_final)


def kernel(x, mem, g_mix, w_in, lower_bounds, attn_sinks, g_onorm, w_branch_a, w_branch_b,
           w_mix_out, g_cross, g_mem, w_cq, w_ckv, w_co, g_ffn, w_ffn_in, conv_w, conv_b,
           w_ffn_down, g_final):
    depth = g_mix.shape[0]
    assert depth == 1 and x.shape[-1] == D_MODEL and x.shape[1] % TOKEN_TILE == 0
    tt = TOKEN_TILE
    bf = lambda w: w.astype(_BF16)
    h = x
    for l in range(depth):
        h, (ckv16, cq16, co16, ffn_in16, ffn_down16) = _mixer(
            h, g_mix[l][None], bf(w_in[l]), lower_bounds.astype(_F32), attn_sinks[l],
            g_onorm[l][None], bf(w_branch_a[l]), bf(w_branch_b[l]), bf(w_mix_out[l]),
            [w_ckv[l], w_cq[l], w_co[l], w_ffn_in[l], w_ffn_down[l]], tt)
        kv = _mem_kv(mem, g_mem[l][None], ckv16)
        h = _cross(h, g_cross[l][None], cq16, kv, co16, CROSS_TILE)
        h = _ffn(h, g_ffn[l][None], ffn_in16, conv_w[l], conv_b[l][None], ffn_down16,
                 g_final[None], FFN_TILE)
    return h
```

```python
import functools

import jax
import jax.numpy as jnp
from jax import lax
from jax.experimental import pallas as pl
from jax.experimental.pallas import tpu as pltpu

D_MODEL = 1024
CHUNK = 64
EPS = 1e-6

ATT_HEADS = 8
ATT_KV_HEADS = 2
ATT_HEAD_DIM = 64
ATT_GROUP = ATT_HEADS // ATT_KV_HEADS
LOOKBACK = 2
ATT_Q_W = ATT_HEADS * ATT_HEAD_DIM
ATT_KV_W = ATT_KV_HEADS * ATT_HEAD_DIM
KV_HALO = LOOKBACK * CHUNK
ATT_QBLK = 2 * CHUNK
ATT_KBLK = ATT_QBLK + KV_HALO

HG_HEADS = 4
HG_DIM = 128
HG_W = HG_HEADS * HG_DIM
HG_BLK = 4 * CHUNK

X_HEADS = 4
X_HEAD_DIM = D_MODEL // X_HEADS

D_FF = 2816
CONV_WIDTH = 3
SUBLANES = 8
BF16_ROWS = 16

_OFF_AQ = 0
_OFF_AK = _OFF_AQ + ATT_Q_W
_OFF_AV = _OFF_AK + ATT_KV_W
_OFF_HQ = _OFF_AV + ATT_KV_W
_OFF_HF = _OFF_HQ + HG_W
_OFF_HI = _OFF_HF + HG_W
_OFF_HG = _OFF_HI + HG_W
_OFF_GA = _OFF_HG + HG_W
_OFF_GB = _OFF_GA + D_MODEL
IN_W = _OFF_GB + D_MODEL

TOKEN_TILE = 512
CROSS_TILE = 1024
FFN_TILE = 512
VMEM_LIMIT_BYTES = 56 * 1024 * 1024

_F32 = jnp.float32
_BF16 = jnp.bfloat16
_NT = (((1,), (1,)), ((), ()))


def _rms(xf, gain):
    return xf * lax.rsqrt(jnp.mean(xf * xf, axis=-1, keepdims=True) + EPS) * gain


def _dot(a, b):
    return jnp.dot(a, b, preferred_element_type=_F32)


def _dot_nt(a, b):
    return lax.dot_general(a, b, _NT, preferred_element_type=_F32)


def _chunk_cumsum(a):
    rows, cols = a.shape
    sub = lax.broadcasted_iota(jnp.int32, (SUBLANES, cols), 0)
    groups = []
    for g0 in range(0, rows, SUBLANES):
        g = a[g0:g0 + SUBLANES, :]
        step = 1
        while step < SUBLANES:
            g = g + jnp.where(sub >= step, pltpu.roll(g, step, 0), 0.0)
            step *= 2
        if g0 % CHUNK:
            g = g + carry
        carry = jnp.broadcast_to(g[SUBLANES - 1:SUBLANES, :], (SUBLANES, cols))
        groups.append(g)
    return jnp.concatenate(groups, axis=0)


def _mixer_kernel(x_ref, gmix_ref, win_ref, lbraw_ref, sinks_ref, gon_ref, wa_ref, wb_ref,
                  wmix_ref, *rest, tt, n_later):
    later_f32 = rest[:n_later]
    o_ref = rest[n_later]
    later_bf16 = rest[n_later + 1:2 * n_later + 1]
    (n_ref, proj_ref, kbuf_ref, kswp_ref, vbuf_ref, vswp_ref,
     st_ref, ao_ref, hy_ref) = rest[2 * n_later + 1:]
    t = pl.program_id(1)
    win = KV_HALO + tt

    @pl.when(t == 0)
    def _():
        for ref in (kbuf_ref, kswp_ref, vbuf_ref, vswp_ref):
            ref[0:KV_HALO, :] = jnp.zeros((KV_HALO, ATT_KV_W), _BF16)
        st_ref[...] = jnp.zeros(st_ref.shape, _F32)

    n_ref[...] = _rms(x_ref[...], gmix_ref[...]).astype(_BF16)
    proj_ref[:, 0:_OFF_HQ] = _dot(n_ref[...], win_ref[:, 0:_OFF_HQ])
    proj_ref[:, _OFF_HQ:_OFF_GA] = _dot(n_ref[...], win_ref[:, _OFF_HQ:_OFF_GA])

    new = slice(KV_HALO, win)
    k32 = proj_ref[:, _OFF_AK:_OFF_AK + ATT_KV_W]
    v32 = proj_ref[:, _OFF_AV:_OFF_AV + ATT_KV_W]
    kbuf_ref[new, :] = k32.astype(_BF16)
    vbuf_ref[new, :] = v32.astype(_BF16)
    kswp_ref[new, :] = pltpu.roll(k32, ATT_HEAD_DIM, 1).astype(_BF16)
    vswp_ref[new, :] = pltpu.roll(v32, ATT_HEAD_DIM, 1).astype(_BF16)

    kvlane = lax.broadcasted_iota(jnp.int32, (win, ATT_KV_W), 1)
    halves = (kvlane < ATT_HEAD_DIM, kvlane >= ATT_HEAD_DIM)
    zero_kv = jnp.zeros((win, ATT_KV_W), _BF16)

    def half_placed(nat_ref, swp_ref, g, par):
        src = nat_ref if g == par else swp_ref
        return jnp.where(halves[par], src[...], zero_kv)

    kz = [[half_placed(kbuf_ref, kswp_ref, g, par) for par in range(2)]
          for g in range(ATT_KV_HEADS)]
    vz = [[half_placed(vbuf_ref, vswp_ref, g, par) for par in range(2)]
          for g in range(ATT_KV_HEADS)]
    qi = lax.broadcasted_iota(jnp.int32, (ATT_QBLK, ATT_KBLK), 0)
    kj = lax.broadcasted_iota(jnp.int32, (ATT_QBLK, ATT_KBLK), 1)
    qc = lax.shift_right_logical(qi, 6)
    kc = lax.shift_right_logical(kj, 6)
    band = (kc >= qc) & (kc <= qc + LOOKBACK)
    absdist = jnp.abs(qi + KV_HALO - kj).astype(_F32)
    q16 = proj_ref[:, _OFF_AQ:_OFF_AQ + ATT_Q_W].astype(_BF16)
    nblk = tt // ATT_QBLK
    probs = [[None] * ATT_HEADS for _ in range(nblk)]
    inv_den = [[None] * ATT_HEADS for _ in range(nblk)]
    for blk in range(nblk):
        r0 = blk * ATT_QBLK
        allowed = band & ((t * tt + kj) >= KV_HALO) if blk == 0 else band
        for h in range(ATT_HEADS):
            g, par, pair = h // ATT_GROUP, h % 2, h // 2
            qpair = q16[r0:r0 + ATT_QBLK, 2 * ATT_HEAD_DIM * pair:2 * ATT_HEAD_DIM * (pair + 1)]
            s = _dot_nt(qpair, kz[g][par][r0:r0 + ATT_KBLK])
            s = jnp.where(allowed,
                          s * (ATT_HEAD_DIM ** -0.5) - (2.0 ** -(h + 1)) * absdist, -jnp.inf)
            sink = sinks_ref[h]
            m = jnp.maximum(jnp.max(s, axis=-1, keepdims=True), sink)
            p = jnp.exp(s - m)
            inv_den[blk][h] = 1.0 / (jnp.sum(p, axis=-1, keepdims=True) + jnp.exp(sink - m))
            probs[blk][h] = p.astype(_BF16)

    proj_ref[:, _OFF_GA:IN_W] = _dot(n_ref[...], win_ref[:, _OFF_GA:IN_W])

    lraw = lbraw_ref[...]
    lexp = jnp.exp(lraw - jnp.max(lraw, axis=0, keepdims=True))
    lb = lexp[0:1, :] / jnp.sum(lexp, axis=0, keepdims=True)
    ri = lax.broadcasted_iota(jnp.int32, (HG_BLK, HG_BLK), 0)
    ci = lax.broadcasted_iota(jnp.int32, (HG_BLK, HG_BLK), 1)
    chunk_causal = (lax.shift_right_logical(ri, 6) == lax.shift_right_logical(ci, 6)) & (ri >= ci)
    rowchunk = lax.shift_right_logical(lax.broadcasted_iota(jnp.int32, (HG_BLK, HG_DIM), 0), 6)
    zero_hd = jnp.zeros((HG_BLK, HG_DIM), _BF16)
    ncb = HG_BLK // CHUNK
    hg_rows = [slice(r0, r0 + HG_BLK) for r0 in range(0, tt, HG_BLK)]

    def chunk_blocks(a):
        return jnp.concatenate([jnp.where(rowchunk == c, a, zero_hd) for c in range(ncb)], axis=1)

    def decayed_operands(rows):
        f = lb + (1.0 - lb) * jax.nn.sigmoid(proj_ref[rows, _OFF_HF:_OFF_HF + HG_W])
        b = _chunk_cumsum(jnp.log(f))
        b_last = [b[(c + 1) * CHUNK - 1:(c + 1) * CHUNK, :] for c in range(ncb)]
        b_last_rows = jnp.concatenate(
            [jnp.broadcast_to(bl, (CHUNK, HG_W)) for bl in b_last], axis=0)
        q_dec = (proj_ref[rows, _OFF_HQ:_OFF_HQ + HG_W] * (HG_DIM ** -0.5)
                 * jnp.exp(b)).astype(_BF16)
        k_inv = ((1.0 - f) * jnp.exp(-b)).astype(_BF16)
        k_end = ((1.0 - f) * jnp.exp(b_last_rows - b)).astype(_BF16)
        return q_dec, k_inv, k_end, [jnp.exp(bl) for bl in b_last]

    operands = [decayed_operands(rows) for rows in hg_rows]

    olane = lax.broadcasted_iota(jnp.int32, (ATT_QBLK, 2 * ATT_HEAD_DIM), 1)
    for blk in range(nblk):
        r0 = blk * ATT_QBLK
        for pair in range(ATT_HEADS // 2):
            g, he, ho = (2 * pair) // ATT_GROUP, 2 * pair, 2 * pair + 1
            o2 = (_dot(probs[blk][he], vz[g][0][r0:r0 + ATT_KBLK])
                  + _dot(probs[blk][ho], vz[g][1][r0:r0 + ATT_KBLK]))
            scale2 = jnp.where(olane < ATT_HEAD_DIM, inv_den[blk][he], inv_den[blk][ho])
            ao_ref[r0:r0 + ATT_QBLK, 2 * ATT_HEAD_DIM * pair:2 * ATT_HEAD_DIM * (pair + 1)] = (
                (o2 * scale2).astype(_BF16))

    def state_free_dots(rows, q_dec, k_inv, k_end):
        a_mats, ds_all, v16 = [], [], []
        for hh in range(HG_HEADS):
            sl = slice(hh * HG_DIM, (hh + 1) * HG_DIM)
            a_mats.append(
                jnp.where(chunk_causal, _dot_nt(q_dec[:, sl], k_inv[:, sl]), 0.0).astype(_BF16))
            v = proj_ref[rows, _OFF_HI + hh * HG_DIM:_OFF_HI + (hh + 1) * HG_DIM]
            v16.append(v.astype(_BF16))
            ds_all.append(_dot(v.T.astype(_BF16), chunk_blocks(k_end[:, sl])))
        return a_mats, ds_all, v16

    local = [state_free_dots(rows, *ops[:3]) for rows, ops in zip(hg_rows, operands)]

    ya = _dot(ao_ref[...], wa_ref[...])

    gon = gon_ref[...]
    state = [st_ref[hh] for hh in range(HG_HEADS)]
    for rows, (q_dec, _, _, decay), (a_mats, ds_all, v16) in zip(hg_rows, operands, local):
        for hh in range(HG_HEADS):
            sl = slice(hh * HG_DIM, (hh + 1) * HG_DIM)
            st = state[hh]
            entering = []
            for c in range(ncb):
                entering.append(st.astype(_BF16))
                st = st * decay[c][:, sl] + ds_all[hh][:, c * HG_DIM:(c + 1) * HG_DIM]
            state[hh] = st
            o = (_dot(a_mats[hh], v16[hh])
                 + _dot_nt(chunk_blocks(q_dec[:, sl]), jnp.concatenate(entering, axis=1)))
            gate = jax.nn.silu(proj_ref[rows, _OFF_HG + hh * HG_DIM:_OFF_HG + (hh + 1) * HG_DIM])
            y = o * lax.rsqrt(jnp.mean(o * o, axis=-1, keepdims=True) + EPS) * gon * gate
            hy_ref[rows, sl] = y.astype(_BF16)
    for hh in range(HG_HEADS):
        st_ref[hh] = state[hh]

    yb = _dot(hy_ref[...], wb_ref[...])
    y = (jax.nn.sigmoid(proj_ref[:, _OFF_GA:_OFF_GA + D_MODEL]) * ya
         + jax.nn.sigmoid(proj_ref[:, _OFF_GB:_OFF_GB + D_MODEL]) * yb)
    o_ref[...] = x_ref[...] + _dot(y.astype(_BF16), wmix_ref[...])

    for ref in (kbuf_ref, kswp_ref, vbuf_ref, vswp_ref):
        ref[0:KV_HALO, :] = ref[tt:tt + KV_HALO, :]
    for src, dst in zip(later_f32, later_bf16):
        dst[...] = src[...].astype(_BF16)


def _const_spec(shape):
    return pl.BlockSpec(shape, lambda b, t: (0,) * len(shape), pipeline_mode=pl.Buffered(1))


def _slab_spec(shape, nt, nsteps):
    rows, cols = shape
    slab = next(s for s in range(BF16_ROWS, rows + 1, BF16_ROWS)
                if rows % s == 0 and rows // s <= nsteps)
    nslabs = rows // slab
    return pl.BlockSpec((slab, cols), lambda b, t: ((b * nt + t) * nslabs // nsteps, 0))


def _mixer(x, g_mix, w_in, lb_raw, sinks, g_onorm, w_a, w_b, w_mix, later_weights, tt):
    bsz, seq, d = x.shape
    nt = seq // tt
    tile = pl.BlockSpec((None, tt, d), lambda b, t: (b, t, 0))
    slabs = [_slab_spec(w.shape, nt, bsz * nt) for w in later_weights]
    h, *later16 = pl.pallas_call(
        functools.partial(_mixer_kernel, tt=tt, n_later=len(later_weights)),
        grid=(bsz, nt),
        in_specs=[
            tile,
            _const_spec(g_mix.shape),
            _const_spec(w_in.shape),
            _const_spec(lb_raw.shape),
            pl.BlockSpec(memory_space=pltpu.SMEM),
            _const_spec(g_onorm.shape),
            _const_spec(w_a.shape),
            _const_spec(w_b.shape),
            _const_spec(w_mix.shape),
        ] + slabs,
        out_specs=[tile] + slabs,
        out_shape=[jax.ShapeDtypeStruct(x.shape, _F32)]
        + [jax.ShapeDtypeStruct(w.shape, _BF16) for w in later_weights],
        scratch_shapes=[
            pltpu.VMEM((tt, d), _BF16),
            pltpu.VMEM((tt, IN_W), _F32),
            pltpu.VMEM((KV_HALO + tt, ATT_KV_W), _BF16),
            pltpu.VMEM((KV_HALO + tt, ATT_KV_W), _BF16),
            pltpu.VMEM((KV_HALO + tt, ATT_KV_W), _BF16),
            pltpu.VMEM((KV_HALO + tt, ATT_KV_W), _BF16),
            pltpu.VMEM((HG_HEADS, HG_DIM, HG_DIM), _F32),
            pltpu.VMEM((tt, ATT_Q_W), _BF16),
            pltpu.VMEM((tt, HG_W), _BF16),
        ],
        compiler_params=pltpu.CompilerParams(
            dimension_semantics=("arbitrary", "arbitrary"),
            vmem_limit_bytes=VMEM_LIMIT_BYTES),
        name="mixer",
    )(x, g_mix, w_in, lb_raw, sinks, g_onorm, w_a, w_b, w_mix, *later_weights)
    return h, later16


def _mem_kv_kernel(mem_ref, gmem_ref, wkv_ref, kv_ref):
    nm = _rms(mem_ref[...], gmem_ref[...]).astype(_BF16)
    kv_ref[...] = _dot(nm, wkv_ref[...]).astype(_BF16)


def _mem_kv(mem, g_mem, w_ckv):
    bsz, m, d = mem.shape
    return pl.pallas_call(
        _mem_kv_kernel,
        grid=(bsz,),
        in_specs=[
            pl.BlockSpec((None, m, d), lambda b: (b, 0, 0)),
            pl.BlockSpec(g_mem.shape, lambda b: (0, 0)),
            pl.BlockSpec(w_ckv.shape, lambda b: (0, 0)),
        ],
        out_specs=pl.BlockSpec((None, m, 2 * d), lambda b: (b, 0, 0)),
        out_shape=jax.ShapeDtypeStruct((bsz, m, 2 * d), _BF16),
        compiler_params=pltpu.CompilerParams(
            dimension_semantics=("arbitrary",), vmem_limit_bytes=VMEM_LIMIT_BYTES),
        name="mem_kv",
    )(mem, g_mem, w_ckv)


def _cross_kernel(h_ref, gc_ref, wq_ref, kv_ref, wo_ref, o_ref):
    h = h_ref[...]
    q = _dot(_rms(h, gc_ref[...]).astype(_BF16), wq_ref[...]).astype(_BF16)
    heads = []
    for hd in range(X_HEADS):
        ks = slice(hd * X_HEAD_DIM, (hd + 1) * X_HEAD_DIM)
        vs = slice(D_MODEL + hd * X_HEAD_DIM, D_MODEL + (hd + 1) * X_HEAD_DIM)
        s = _dot_nt(q[:, ks], kv_ref[:, ks]) * (X_HEAD_DIM ** -0.5)
        p = jnp.exp(s - jnp.max(s, axis=-1, keepdims=True))
        den = jnp.sum(p, axis=-1, keepdims=True)
        heads.append((_dot(p.astype(_BF16), kv_ref[:, vs]) / den).astype(_BF16))
    o_ref[...] = h + _dot(jnp.concatenate(heads, axis=-1), wo_ref[...])


def _cross(h, g_cross, w_cq, kv, w_co, tt):
    bsz, seq, d = h.shape
    m = kv.shape[1]
    tile = pl.BlockSpec((None, tt, d), lambda b, t: (b, t, 0))
    return pl.pallas_call(
        _cross_kernel,
        grid=(bsz, seq // tt),
        in_specs=[
            tile,
            _const_spec(g_cross.shape),
            _const_spec(w_cq.shape),
            pl.BlockSpec((None, m, 2 * d), lambda b, t: (b, 0, 0)),
            _const_spec(w_co.shape),
        ],
        out_specs=tile,
        out_shape=jax.ShapeDtypeStruct(h.shape, _F32),
        compiler_params=pltpu.CompilerParams(
            dimension_semantics=("arbitrary", "arbitrary"),
            vmem_limit_bytes=VMEM_LIMIT_BYTES),
        name="cross",
    )(h, g_cross, w_cq, kv, w_co)


def _shift_rows(a, first_row):
    rolled = pltpu.roll(a, 1, 0)
    row = lax.broadcasted_iota(jnp.int32, (SUBLANES, a.shape[1]), 0)
    head = jnp.where(row == 0, first_row, rolled[0:SUBLANES])
    return jnp.concatenate([head, rolled[SUBLANES:]], axis=0)


def _ffn_kernel(h_ref, gf_ref, win_ref, cw_ref, cb_ref, wd_ref, gfin_ref, o_ref, tail_ref, *, tt):
    t = pl.program_id(1)

    @pl.when(t == 0)
    def _():
        tail_ref[...] = jnp.zeros(tail_ref.shape, _F32)

    h = h_ref[...]
    n = _rms(h, gf_ref[...]).astype(_BF16)
    u = _dot(n, win_ref[:, 0:D_FF])
    gate = _dot(n, win_ref[:, D_FF:2 * D_FF])
    w0, w1, w2 = cw_ref[0:1, :], cw_ref[1:2, :], cw_ref[2:3, :]
    prev1 = tail_ref[SUBLANES - 1:SUBLANES, :]
    prev2 = tail_ref[SUBLANES - 2:SUBLANES - 1, :]
    acc = w1 * u + _shift_rows(w0 * u, w0 * prev1)
    conv = w2 * u + _shift_rows(acc, w1 * prev1 + w0 * prev2) + cb_ref[...]
    tail_ref[...] = u[tt - SUBLANES:tt, :]
    act = (jax.nn.silu(conv) * gate).astype(_BF16)
    h3 = h + _dot(act, wd_ref[...])
    o_ref[...] = _rms(h3, gfin_ref[...])


def _ffn(h, g_ffn, w_ffn_in, conv_w, conv_b, w_down, g_final, tt):
    bsz, seq, d = h.shape
    tile = pl.BlockSpec((None, tt, d), lambda b, t: (b, t, 0))
    return pl.pallas_call(
        functools.partial(_ffn_kernel, tt=tt),
        grid=(bsz, seq // tt),
        in_specs=[
            tile,
            _const_spec(g_ffn.shape),
            _const_spec(w_ffn_in.shape),
            _const_spec(conv_w.shape),
            _const_spec(conv_b.shape),
            _const_spec(w_down.shape),
            _const_spec(g_final.shape),
        ],
        out_specs=tile,
        out_shape=jax.ShapeDtypeStruct(h.shape, _F32),
        scratch_shapes=[pltpu.VMEM((SUBLANES, D_FF), _F32)],
        compiler_params=pltpu.CompilerParams(
            dimension_semantics=("arbitrary", "arbitrary"),
            vmem_limit_bytes=VMEM_LIMIT_BYTES),
        name="ffn",
    )(h, g_ffn, w_ffn_in, conv_w, conv_b, w_down, g_final)


def kernel(x, mem, g_mix, w_in, lower_bounds, attn_sinks, g_onorm, w_branch_a, w_branch_b,
           w_mix_out, g_cross, g_mem, w_cq, w_ckv, w_co, g_ffn, w_ffn_in, conv_w, conv_b,
           w_ffn_down, g_final):
    depth = g_mix.shape[0]
    assert depth == 1 and x.shape[-1] == D_MODEL and x.shape[1] % TOKEN_TILE == 0
    tt = TOKEN_TILE
    bf = lambda w: w.astype(_BF16)
    h = x
    for l in range(depth):
        h, (ckv16, cq16, co16, ffn_in16, ffn_down16) = _mixer(
            h, g_mix[l][None], bf(w_in[l]), lower_bounds.astype(_F32), attn_sinks[l],
            g_onorm[l][None], bf(w_branch_a[l]), bf(w_branch_b[l]), bf(w_mix_out[l]),
            [w_ckv[l], w_cq[l], w_co[l], w_ffn_in[l], w_ffn_down[l]], tt)
        kv = _mem_kv(mem, g_mem[l][None], ckv16)
        h = _cross(h, g_cross[l][None], cq16, kv, co16, CROSS_TILE)
        h = _ffn(h, g_ffn[l][None], ffn_in16, conv_w[l], conv_b[l][None], ffn_down16,
                 g_final[None], FFN_TILE)
    return h
```

```python
import functools

import jax
import jax.numpy as jnp
from jax import lax
from jax.experimental import pallas as pl
from jax.experimental.pallas import tpu as pltpu

D_MODEL = 1024
CHUNK = 64
EPS = 1e-6

ATT_HEADS = 8
ATT_KV_HEADS = 2
ATT_HEAD_DIM = 64
ATT_GROUP = ATT_HEADS // ATT_KV_HEADS
LOOKBACK = 2
ATT_Q_W = ATT_HEADS * ATT_HEAD_DIM
ATT_KV_W = ATT_KV_HEADS * ATT_HEAD_DIM
KV_HALO = LOOKBACK * CHUNK
ATT_QBLK = 2 * CHUNK
ATT_KBLK = ATT_QBLK + KV_HALO

HG_HEADS = 4
HG_DIM = 128
HG_W = HG_HEADS * HG_DIM
HG_BLK = 4 * CHUNK

X_HEADS = 4
X_HEAD_DIM = D_MODEL // X_HEADS

D_FF = 2816
CONV_WIDTH = 3
FF_COLS = 256
FF_OUT_ROWS = 256
SUBLANES = 8
BF16_ROWS = 16

_OFF_AQ = 0
_OFF_AK = _OFF_AQ + ATT_Q_W
_OFF_AV = _OFF_AK + ATT_KV_W
_OFF_HQ = _OFF_AV + ATT_KV_W
_OFF_HF = _OFF_HQ + HG_W
_OFF_HI = _OFF_HF + HG_W
_OFF_HG = _OFF_HI + HG_W
_OFF_GA = _OFF_HG + HG_W
_OFF_GB = _OFF_GA + D_MODEL
IN_W = _OFF_GB + D_MODEL

TOKEN_TILE = 512
CROSS_TILE = 1024
FFN_TILE = 1024
VMEM_LIMIT_BYTES = 56 * 1024 * 1024

_F32 = jnp.float32
_BF16 = jnp.bfloat16
_NT = (((1,), (1,)), ((), ()))


def _rms(xf, gain):
    return xf * lax.rsqrt(jnp.mean(xf * xf, axis=-1, keepdims=True) + EPS) * gain


def _dot(a, b):
    return jnp.dot(a, b, preferred_element_type=_F32)


def _dot_nt(a, b):
    return lax.dot_general(a, b, _NT, preferred_element_type=_F32)


def _chunk_cumsum(a):
    rows, cols = a.shape
    sub = lax.broadcasted_iota(jnp.int32, (SUBLANES, cols), 0)
    groups = []
    for g0 in range(0, rows, SUBLANES):
        g = a[g0:g0 + SUBLANES, :]
        step = 1
        while step < SUBLANES:
            g = g + jnp.where(sub >= step, pltpu.roll(g, step, 0), 0.0)
            step *= 2
        if g0 % CHUNK:
            g = g + carry
        carry = jnp.broadcast_to(g[SUBLANES - 1:SUBLANES, :], (SUBLANES, cols))
        groups.append(g)
    return jnp.concatenate(groups, axis=0)


def _mixer_kernel(x_ref, gmix_ref, win_ref, lbraw_ref, sinks_ref, gon_ref, wa_ref, wb_ref,
                  wmix_ref, *rest, tt, n_later):
    later_f32 = rest[:n_later]
    o_ref = rest[n_later]
    later_bf16 = rest[n_later + 1:2 * n_later + 1]
    (n_ref, proj_ref, kbuf_ref, kswp_ref, vbuf_ref, vswp_ref,
     st_ref, ao_ref, hy_ref) = rest[2 * n_later + 1:]
    t = pl.program_id(1)
    win = KV_HALO + tt

    @pl.when(t == 0)
    def _():
        for ref in (kbuf_ref, kswp_ref, vbuf_ref, vswp_ref):
            ref[0:KV_HALO, :] = jnp.zeros((KV_HALO, ATT_KV_W), _BF16)
        st_ref[...] = jnp.zeros(st_ref.shape, _F32)

    n_ref[...] = _rms(x_ref[...], gmix_ref[...]).astype(_BF16)
    proj_ref[:, 0:_OFF_HQ] = _dot(n_ref[...], win_ref[:, 0:_OFF_HQ])
    proj_ref[:, _OFF_HQ:_OFF_GA] = _dot(n_ref[...], win_ref[:, _OFF_HQ:_OFF_GA])

    new = slice(KV_HALO, win)
    k32 = proj_ref[:, _OFF_AK:_OFF_AK + ATT_KV_W]
    v32 = proj_ref[:, _OFF_AV:_OFF_AV + ATT_KV_W]
    kbuf_ref[new, :] = k32.astype(_BF16)
    vbuf_ref[new, :] = v32.astype(_BF16)
    kswp_ref[new, :] = pltpu.roll(k32, ATT_HEAD_DIM, 1).astype(_BF16)
    vswp_ref[new, :] = pltpu.roll(v32, ATT_HEAD_DIM, 1).astype(_BF16)

    kvlane = lax.broadcasted_iota(jnp.int32, (win, ATT_KV_W), 1)
    halves = (kvlane < ATT_HEAD_DIM, kvlane >= ATT_HEAD_DIM)
    zero_kv = jnp.zeros((win, ATT_KV_W), _BF16)

    def half_placed(nat_ref, swp_ref, g, par):
        src = nat_ref if g == par else swp_ref
        return jnp.where(halves[par], src[...], zero_kv)

    kz = [[half_placed(kbuf_ref, kswp_ref, g, par) for par in range(2)]
          for g in range(ATT_KV_HEADS)]
    vz = [[half_placed(vbuf_ref, vswp_ref, g, par) for par in range(2)]
          for g in range(ATT_KV_HEADS)]
    qi = lax.broadcasted_iota(jnp.int32, (ATT_QBLK, ATT_KBLK), 0)
    kj = lax.broadcasted_iota(jnp.int32, (ATT_QBLK, ATT_KBLK), 1)
    qc = lax.shift_right_logical(qi, 6)
    kc = lax.shift_right_logical(kj, 6)
    band = (kc >= qc) & (kc <= qc + LOOKBACK)
    absdist = jnp.abs(qi + KV_HALO - kj).astype(_F32)
    q16 = proj_ref[:, _OFF_AQ:_OFF_AQ + ATT_Q_W].astype(_BF16)
    nblk = tt // ATT_QBLK
    probs = [[None] * ATT_HEADS for _ in range(nblk)]
    inv_den = [[None] * ATT_HEADS for _ in range(nblk)]
    for blk in range(nblk):
        r0 = blk * ATT_QBLK
        allowed = band & ((t * tt + kj) >= KV_HALO) if blk == 0 else band
        for h in range(ATT_HEADS):
            g, par, pair = h // ATT_GROUP, h % 2, h // 2
            qpair = q16[r0:r0 + ATT_QBLK, 2 * ATT_HEAD_DIM * pair:2 * ATT_HEAD_DIM * (pair + 1)]
            s = _dot_nt(qpair, kz[g][par][r0:r0 + ATT_KBLK])
            s = jnp.where(allowed,
                          s * (ATT_HEAD_DIM ** -0.5) - (2.0 ** -(h + 1)) * absdist, -jnp.inf)
            sink = sinks_ref[h]
            m = jnp.maximum(jnp.max(s, axis=-1, keepdims=True), sink)
            p = jnp.exp(s - m)
            inv_den[blk][h] = 1.0 / (jnp.sum(p, axis=-1, keepdims=True) + jnp.exp(sink - m))
            probs[blk][h] = p.astype(_BF16)

    proj_ref[:, _OFF_GA:IN_W] = _dot(n_ref[...], win_ref[:, _OFF_GA:IN_W])

    lraw = lbraw_ref[...]
    lexp = jnp.exp(lraw - jnp.max(lraw, axis=0, keepdims=True))
    lb = lexp[0:1, :] / jnp.sum(lexp, axis=0, keepdims=True)
    ri = lax.broadcasted_iota(jnp.int32, (HG_BLK, HG_BLK), 0)
    ci = lax.broadcasted_iota(jnp.int32, (HG_BLK, HG_BLK), 1)
    chunk_causal = (lax.shift_right_logical(ri, 6) == lax.shift_right_logical(ci, 6)) & (ri >= ci)
    rowchunk = lax.shift_right_logical(lax.broadcasted_iota(jnp.int32, (HG_BLK, HG_DIM), 0), 6)
    zero_hd = jnp.zeros((HG_BLK, HG_DIM), _BF16)
    ncb = HG_BLK // CHUNK
    hg_rows = [slice(r0, r0 + HG_BLK) for r0 in range(0, tt, HG_BLK)]

    def chunk_blocks(a):
        return jnp.concatenate([jnp.where(rowchunk == c, a, zero_hd) for c in range(ncb)], axis=1)

    def decayed_operands(rows):
        f = lb + (1.0 - lb) * jax.nn.sigmoid(proj_ref[rows, _OFF_HF:_OFF_HF + HG_W])
        b = _chunk_cumsum(jnp.log(f))
        b_last = [b[(c + 1) * CHUNK - 1:(c + 1) * CHUNK, :] for c in range(ncb)]
        b_last_rows = jnp.concatenate(
            [jnp.broadcast_to(bl, (CHUNK, HG_W)) for bl in b_last], axis=0)
        q_dec = (proj_ref[rows, _OFF_HQ:_OFF_HQ + HG_W] * (HG_DIM ** -0.5)
                 * jnp.exp(b)).astype(_BF16)
        k_inv = ((1.0 - f) * jnp.exp(-b)).astype(_BF16)
        k_end = ((1.0 - f) * jnp.exp(b_last_rows - b)).astype(_BF16)
        return q_dec, k_inv, k_end, [jnp.exp(bl) for bl in b_last]

    operands = [decayed_operands(rows) for rows in hg_rows]

    olane = lax.broadcasted_iota(jnp.int32, (ATT_QBLK, 2 * ATT_HEAD_DIM), 1)
    for blk in range(nblk):
        r0 = blk * ATT_QBLK
        for pair in range(ATT_HEADS // 2):
            g, he, ho = (2 * pair) // ATT_GROUP, 2 * pair, 2 * pair + 1
            o2 = (_dot(probs[blk][he], vz[g][0][r0:r0 + ATT_KBLK])
                  + _dot(probs[blk][ho], vz[g][1][r0:r0 + ATT_KBLK]))
            scale2 = jnp.where(olane < ATT_HEAD_DIM, inv_den[blk][he], inv_den[blk][ho])
            ao_ref[r0:r0 + ATT_QBLK, 2 * ATT_HEAD_DIM * pair:2 * ATT_HEAD_DIM * (pair + 1)] = (
                (o2 * scale2).astype(_BF16))

    def state_free_dots(rows, q_dec, k_inv, k_end):
        a_mats, ds_all, v16 = [], [], []
        for hh in range(HG_HEADS):
            sl = slice(hh * HG_DIM, (hh + 1) * HG_DIM)
            a_mats.append(
                jnp.where(chunk_causal, _dot_nt(q_dec[:, sl], k_inv[:, sl]), 0.0).astype(_BF16))
            v = proj_ref[rows, _OFF_HI + hh * HG_DIM:_OFF_HI + (hh + 1) * HG_DIM]
            v16.append(v.astype(_BF16))
            ds_all.append(_dot(v.T.astype(_BF16), chunk_blocks(k_end[:, sl])))
        return a_mats, ds_all, v16

    local = [state_free_dots(rows, *ops[:3]) for rows, ops in zip(hg_rows, operands)]

    ya = _dot(ao_ref[...], wa_ref[...])

    gon = gon_ref[...]
    state = [st_ref[hh] for hh in range(HG_HEADS)]
    for rows, (q_dec, _, _, decay), (a_mats, ds_all, v16) in zip(hg_rows, operands, local):
        for hh in range(HG_HEADS):
            sl = slice(hh * HG_DIM, (hh + 1) * HG_DIM)
            st = state[hh]
            entering = []
            for c in range(ncb):
                entering.append(st.astype(_BF16))
                st = st * decay[c][:, sl] + ds_all[hh][:, c * HG_DIM:(c + 1) * HG_DIM]
            state[hh] = st
            o = (_dot(a_mats[hh], v16[hh])
                 + _dot_nt(chunk_blocks(q_dec[:, sl]), jnp.concatenate(entering, axis=1)))
            gate = jax.nn.silu(proj_ref[rows, _OFF_HG + hh * HG_DIM:_OFF_HG + (hh + 1) * HG_DIM])
            y = o * lax.rsqrt(jnp.mean(o * o, axis=-1, keepdims=True) + EPS) * gon * gate
            hy_ref[rows, sl] = y.astype(_BF16)
    for hh in range(HG_HEADS):
        st_ref[hh] = state[hh]

    yb = _dot(hy_ref[...], wb_ref[...])
    y = (jax.nn.sigmoid(proj_ref[:, _OFF_GA:_OFF_GA + D_MODEL]) * ya
         + jax.nn.sigmoid(proj_ref[:, _OFF_GB:_OFF_GB + D_MODEL]) * yb)
    o_ref[...] = x_ref[...] + _dot(y.astype(_BF16), wmix_ref[...])

    for ref in (kbuf_ref, kswp_ref, vbuf_ref, vswp_ref):
        ref[0:KV_HALO, :] = ref[tt:tt + KV_HALO, :]
    _cast_slabs(later_f32, later_bf16)


def _const_spec(shape):
    return pl.BlockSpec(shape, lambda b, t: (0,) * len(shape), pipeline_mode=pl.Buffered(1))


def _slab_spec(shape, nsteps, linear_step):
    rows, cols = shape
    slab = next(s for s in range(BF16_ROWS, rows + 1, BF16_ROWS)
                if rows % s == 0 and rows // s <= nsteps)
    nslabs = rows // slab
    return pl.BlockSpec((slab, cols), lambda *ids: (linear_step(*ids) * nslabs // nsteps, 0))


def _cast_slabs(later_f32, later_bf16):
    for src, dst in zip(later_f32, later_bf16):
        dst[...] = src[...].astype(_BF16)


def _mixer(x, g_mix, w_in, lb_raw, sinks, g_onorm, w_a, w_b, w_mix, later_weights, tt):
    bsz, seq, d = x.shape
    nt = seq // tt
    tile = pl.BlockSpec((None, tt, d), lambda b, t: (b, t, 0))
    slabs = [_slab_spec(w.shape, bsz * nt, lambda b, t: b * nt + t) for w in later_weights]
    h, *later16 = pl.pallas_call(
        functools.partial(_mixer_kernel, tt=tt, n_later=len(later_weights)),
        grid=(bsz, nt),
        in_specs=[
            tile,
            _const_spec(g_mix.shape),
            _const_spec(w_in.shape),
            _const_spec(lb_raw.shape),
            pl.BlockSpec(memory_space=pltpu.SMEM),
            _const_spec(g_onorm.shape),
            _const_spec(w_a.shape),
            _const_spec(w_b.shape),
            _const_spec(w_mix.shape),
        ] + slabs,
        out_specs=[tile] + slabs,
        out_shape=[jax.ShapeDtypeStruct(x.shape, _F32)]
        + [jax.ShapeDtypeStruct(w.shape, _BF16) for w in later_weights],
        scratch_shapes=[
            pltpu.VMEM((tt, d), _BF16),
            pltpu.VMEM((tt, IN_W), _F32),
            pltpu.VMEM((KV_HALO + tt, ATT_KV_W), _BF16),
            pltpu.VMEM((KV_HALO + tt, ATT_KV_W), _BF16),
            pltpu.VMEM((KV_HALO + tt, ATT_KV_W), _BF16),
            pltpu.VMEM((KV_HALO + tt, ATT_KV_W), _BF16),
            pltpu.VMEM((HG_HEADS, HG_DIM, HG_DIM), _F32),
            pltpu.VMEM((tt, ATT_Q_W), _BF16),
            pltpu.VMEM((tt, HG_W), _BF16),
        ],
        compiler_params=pltpu.CompilerParams(
            dimension_semantics=("arbitrary", "arbitrary"),
            vmem_limit_bytes=VMEM_LIMIT_BYTES),
        name="mixer",
    )(x, g_mix, w_in, lb_raw, sinks, g_onorm, w_a, w_b, w_mix, *later_weights)
    return h, later16


def _mem_kv_kernel(mem_ref, gmem_ref, wkv_ref, *rest, n_later):
    later_f32 = rest[:n_later]
    kv_ref = rest[n_later]
    later_bf16 = rest[n_later + 1:2 * n_later + 1]
    wkv16_ref, = rest[2 * n_later + 1:]

    @pl.when(pl.program_id(0) == 0)
    def _():
        wkv16_ref[...] = wkv_ref[...].astype(_BF16)

    nm = _rms(mem_ref[...], gmem_ref[...]).astype(_BF16)
    kv_ref[...] = _dot(nm, wkv16_ref[...]).astype(_BF16)
    _cast_slabs(later_f32, later_bf16)


def _mem_kv(mem, g_mem, w_ckv, later_weights):
    bsz, m, d = mem.shape
    slabs = [_slab_spec(w.shape, bsz, lambda b: b) for w in later_weights]
    kv, *later16 = pl.pallas_call(
        functools.partial(_mem_kv_kernel, n_later=len(later_weights)),
        grid=(bsz,),
        in_specs=[
            pl.BlockSpec((None, m, d), lambda b: (b, 0, 0)),
            pl.BlockSpec(g_mem.shape, lambda b: (0, 0)),
            pl.BlockSpec(w_ckv.shape, lambda b: (0, 0), pipeline_mode=pl.Buffered(1)),
        ] + slabs,
        out_specs=[pl.BlockSpec((None, m, 2 * d), lambda b: (b, 0, 0))] + slabs,
        out_shape=[jax.ShapeDtypeStruct((bsz, m, 2 * d), _BF16)]
        + [jax.ShapeDtypeStruct(w.shape, _BF16) for w in later_weights],
        scratch_shapes=[pltpu.VMEM(w_ckv.shape, _BF16)],
        compiler_params=pltpu.CompilerParams(
            dimension_semantics=("arbitrary",), vmem_limit_bytes=VMEM_LIMIT_BYTES),
        name="mem_kv",
    )(mem, g_mem, w_ckv, *later_weights)
    return kv, later16


def _cross_kernel(h_ref, gc_ref, wq_ref, kv_ref, wo_ref, o_ref):
    h = h_ref[...]
    q = _dot(_rms(h, gc_ref[...]).astype(_BF16), wq_ref[...]).astype(_BF16)
    heads = []
    for hd in range(X_HEADS):
        ks = slice(hd * X_HEAD_DIM, (hd + 1) * X_HEAD_DIM)
        vs = slice(D_MODEL + hd * X_HEAD_DIM, D_MODEL + (hd + 1) * X_HEAD_DIM)
        s = _dot_nt(q[:, ks], kv_ref[:, ks]) * (X_HEAD_DIM ** -0.5)
        p = jnp.exp(s - jnp.max(s, axis=-1, keepdims=True))
        den = jnp.sum(p, axis=-1, keepdims=True)
        heads.append((_dot(p.astype(_BF16), kv_ref[:, vs]) / den).astype(_BF16))
    o_ref[...] = h + _dot(jnp.concatenate(heads, axis=-1), wo_ref[...])


def _cross(h, g_cross, w_cq, kv, w_co, tt):
    bsz, seq, d = h.shape
    m = kv.shape[1]
    tile = pl.BlockSpec((None, tt, d), lambda b, t: (b, t, 0))
    return pl.pallas_call(
        _cross_kernel,
        grid=(bsz, seq // tt),
        in_specs=[
            tile,
            _const_spec(g_cross.shape),
            _const_spec(w_cq.shape),
            pl.BlockSpec((None, m, 2 * d), lambda b, t: (b, 0, 0)),
            _const_spec(w_co.shape),
        ],
        out_specs=tile,
        out_shape=jax.ShapeDtypeStruct(h.shape, _F32),
        compiler_params=pltpu.CompilerParams(
            dimension_semantics=("arbitrary", "arbitrary"),
            vmem_limit_bytes=VMEM_LIMIT_BYTES),
        name="cross",
    )(h, g_cross, w_cq, kv, w_co)


def _shift_rows(a, first_row):
    rolled = pltpu.roll(a, 1, 0)
    row = lax.broadcasted_iota(jnp.int32, (SUBLANES, a.shape[1]), 0)
    head = jnp.where(row == 0, first_row, rolled[0:SUBLANES])
    return jnp.concatenate([head, rolled[SUBLANES:]], axis=0)


def _ffn_kernel(h_ref, gf_ref, win_ref, cw_ref, cb_ref, wd_ref, gfin_ref, o_ref,
                tail_ref, n_ref, act_ref, *, tt):
    t = pl.program_id(1)

    @pl.when(t == 0)
    def _():
        tail_ref[...] = jnp.zeros(tail_ref.shape, _F32)

    n_ref[...] = _rms(h_ref[...], gf_ref[...]).astype(_BF16)
    for c0 in range(0, D_FF, FF_COLS):
        cols = slice(c0, c0 + FF_COLS)
        u = _dot(n_ref[...], win_ref[:, cols])
        gate = _dot(n_ref[...], win_ref[:, D_FF + c0:D_FF + c0 + FF_COLS])
        w0, w1, w2 = cw_ref[0:1, cols], cw_ref[1:2, cols], cw_ref[2:3, cols]
        prev1 = tail_ref[SUBLANES - 1:SUBLANES, cols]
        prev2 = tail_ref[SUBLANES - 2:SUBLANES - 1, cols]
        acc = w1 * u + _shift_rows(w0 * u, w0 * prev1)
        conv = w2 * u + _shift_rows(acc, w1 * prev1 + w0 * prev2) + cb_ref[:, cols]
        tail_ref[:, cols] = u[tt - SUBLANES:tt, :]
        act_ref[:, cols] = (jax.nn.silu(conv) * gate).astype(_BF16)
    for r0 in range(0, tt, FF_OUT_ROWS):
        rows = slice(r0, r0 + FF_OUT_ROWS)
        h3 = h_ref[rows, :] + _dot(act_ref[rows, :], wd_ref[...])
        o_ref[rows, :] = _rms(h3, gfin_ref[...])


def _ffn(h, g_ffn, w_ffn_in, conv_w, conv_b, w_down, g_final, tt):
    bsz, seq, d = h.shape
    tile = pl.BlockSpec((None, tt, d), lambda b, t: (b, t, 0))
    return pl.pallas_call(
        functools.partial(_ffn_kernel, tt=tt),
        grid=(bsz, seq // tt),
        in_specs=[
            tile,
            _const_spec(g_ffn.shape),
            _const_spec(w_ffn_in.shape),
            _const_spec(conv_w.shape),
            _const_spec(conv_b.shape),
            _const_spec(w_down.shape),
            _const_spec(g_final.shape),
        ],
        out_specs=tile,
        out_shape=jax.ShapeDtypeStruct(h.shape, _F32),
        scratch_shapes=[
            pltpu.VMEM((SUBLANES, D_FF), _F32),
            pltpu.VMEM((tt, d), _BF16),
            pltpu.VMEM((tt, D_FF), _BF16),
        ],
        compiler_params=pltpu.CompilerParams(
            dimension_semantics=("arbitrary", "arbitrary"),
            vmem_limit_bytes=VMEM_LIMIT_BYTES),
        name="ffn",
    )(h, g_ffn, w_ffn_in, conv_w, conv_b, w_down, g_final)


def kernel(x, mem, g_mix, w_in, lower_bounds, attn_sinks, g_onorm, w_branch_a, w_branch_b,
           w_mix_out, g_cross, g_mem, w_cq, w_ckv, w_co, g_ffn, w_ffn_in, conv_w, conv_b,
           w_ffn_down, g_final):
    depth = g_mix.shape[0]
    assert depth == 1 and x.shape[-1] == D_MODEL and x.shape[1] % TOKEN_TILE == 0
    tt = TOKEN_TILE
    h = x
    for l in range(depth):
        kv, (in16, a16, b16, mix16) = _mem_kv(
            mem, g_mem[l][None], w_ckv[l], [w_in[l], w_branch_a[l], w_branch_b[l], w_mix_out[l]])
        h, (cq16, co16, ffn_in16, ffn_down16) = _mixer(
            h, g_mix[l][None], in16, lower_bounds.astype(_F32), attn_sinks[l], g_onorm[l][None],
            a16, b16, mix16, [w_cq[l], w_co[l], w_ffn_in[l], w_ffn_down[l]], tt)
        h = _cross(h, g_cross[l][None], cq16, kv, co16, CROSS_TILE)
        h = _ffn(h, g_ffn[l][None], ffn_in16, conv_w[l], conv_b[l][None], ffn_down16,
                 g_final[None], FFN_TILE)
    return h
```

```python
import functools

import jax
import jax.numpy as jnp
from jax import lax
from jax.experimental import pallas as pl
from jax.experimental.pallas import tpu as pltpu

D_MODEL = 1024
CHUNK = 64
EPS = 1e-6

ATT_HEADS = 8
ATT_KV_HEADS = 2
ATT_HEAD_DIM = 64
ATT_GROUP = ATT_HEADS // ATT_KV_HEADS
LOOKBACK = 2
ATT_Q_W = ATT_HEADS * ATT_HEAD_DIM
ATT_KV_W = ATT_KV_HEADS * ATT_HEAD_DIM
KV_HALO = LOOKBACK * CHUNK
ATT_QBLK = 2 * CHUNK
ATT_KBLK = ATT_QBLK + KV_HALO

HG_HEADS = 4
HG_DIM = 128
HG_W = HG_HEADS * HG_DIM
HG_BLK = 4 * CHUNK

X_HEADS = 4
X_HEAD_DIM = D_MODEL // X_HEADS

D_FF = 2816
CONV_WIDTH = 3
FF_COLS = 256
FF_OUT_ROWS = 256
SUBLANES = 8
BF16_ROWS = 16

_OFF_AQ = 0
_OFF_AK = _OFF_AQ + ATT_Q_W
_OFF_AV = _OFF_AK + ATT_KV_W
_OFF_HQ = _OFF_AV + ATT_KV_W
_OFF_HF = _OFF_HQ + HG_W
_OFF_HI = _OFF_HF + HG_W
_OFF_HG = _OFF_HI + HG_W
_OFF_GA = _OFF_HG + HG_W
_OFF_GB = _OFF_GA + D_MODEL
IN_W = _OFF_GB + D_MODEL

TOKEN_TILE = 512
CROSS_TILE = 1024
FFN_TILE = 1024
VMEM_LIMIT_BYTES = 56 * 1024 * 1024

_F32 = jnp.float32
_BF16 = jnp.bfloat16
_NT = (((1,), (1,)), ((), ()))


def _rms(xf, gain):
    return xf * lax.rsqrt(jnp.mean(xf * xf, axis=-1, keepdims=True) + EPS) * gain


def _dot(a, b):
    return jnp.dot(a, b, preferred_element_type=_F32)


def _dot_nt(a, b):
    return lax.dot_general(a, b, _NT, preferred_element_type=_F32)


def _chunk_cumsum(a):
    rows, cols = a.shape
    sub = lax.broadcasted_iota(jnp.int32, (SUBLANES, cols), 0)
    groups = []
    for g0 in range(0, rows, SUBLANES):
        g = a[g0:g0 + SUBLANES, :]
        step = 1
        while step < SUBLANES:
            g = g + jnp.where(sub >= step, pltpu.roll(g, step, 0), 0.0)
            step *= 2
        if g0 % CHUNK:
            g = g + carry
        carry = jnp.broadcast_to(g[SUBLANES - 1:SUBLANES, :], (SUBLANES, cols))
        groups.append(g)
    return jnp.concatenate(groups, axis=0)


def _mixer_kernel(x_ref, gmix_ref, win_ref, lbraw_ref, sinks_ref, gon_ref, wa_ref, wb_ref,
                  wmix_ref, *rest, tt, n_later):
    later_f32 = rest[:n_later]
    o_ref = rest[n_later]
    later_bf16 = rest[n_later + 1:2 * n_later + 1]
    (n_ref, proj_ref, kbuf_ref, kswp_ref, vbuf_ref, vswp_ref,
     st_ref, ao_ref, hy_ref) = rest[2 * n_later + 1:]
    t = pl.program_id(1)
    win = KV_HALO + tt

    @pl.when(t == 0)
    def _():
        for ref in (kbuf_ref, kswp_ref, vbuf_ref, vswp_ref):
            ref[0:KV_HALO, :] = jnp.zeros((KV_HALO, ATT_KV_W), _BF16)
        st_ref[...] = jnp.zeros(st_ref.shape, _F32)

    n_ref[...] = _rms(x_ref[...], gmix_ref[...]).astype(_BF16)
    proj_ref[:, 0:_OFF_HQ] = _dot(n_ref[...], win_ref[:, 0:_OFF_HQ])
    proj_ref[:, _OFF_HQ:_OFF_GA] = _dot(n_ref[...], win_ref[:, _OFF_HQ:_OFF_GA])

    new = slice(KV_HALO, win)
    k32 = proj_ref[:, _OFF_AK:_OFF_AK + ATT_KV_W]
    v32 = proj_ref[:, _OFF_AV:_OFF_AV + ATT_KV_W]
    kbuf_ref[new, :] = k32.astype(_BF16)
    vbuf_ref[new, :] = v32.astype(_BF16)
    kswp_ref[new, :] = pltpu.roll(k32, ATT_HEAD_DIM, 1).astype(_BF16)
    vswp_ref[new, :] = pltpu.roll(v32, ATT_HEAD_DIM, 1).astype(_BF16)

    kvlane = lax.broadcasted_iota(jnp.int32, (win, ATT_KV_W), 1)
    halves = (kvlane < ATT_HEAD_DIM, kvlane >= ATT_HEAD_DIM)
    zero_kv = jnp.zeros((win, ATT_KV_W), _BF16)

    def half_placed(nat_ref, swp_ref, g, par):
        src = nat_ref if g == par else swp_ref
        return jnp.where(halves[par], src[...], zero_kv)

    kz = [[half_placed(kbuf_ref, kswp_ref, g, par) for par in range(2)]
          for g in range(ATT_KV_HEADS)]
    vz = [[half_placed(vbuf_ref, vswp_ref, g, par) for par in range(2)]
          for g in range(ATT_KV_HEADS)]
    qi = lax.broadcasted_iota(jnp.int32, (ATT_QBLK, ATT_KBLK), 0)
    kj = lax.broadcasted_iota(jnp.int32, (ATT_QBLK, ATT_KBLK), 1)
    qc = lax.shift_right_logical(qi, 6)
    kc = lax.shift_right_logical(kj, 6)
    band = (kc >= qc) & (kc <= qc + LOOKBACK)
    absdist = jnp.abs(qi + KV_HALO - kj).astype(_F32)
    q16 = proj_ref[:, _OFF_AQ:_OFF_AQ + ATT_Q_W].astype(_BF16)
    nblk = tt // ATT_QBLK
    probs = [[None] * ATT_HEADS for _ in range(nblk)]
    inv_den = [[None] * ATT_HEADS for _ in range(nblk)]
    for blk in range(nblk):
        r0 = blk * ATT_QBLK
        allowed = band & ((t * tt + kj) >= KV_HALO) if blk == 0 else band
        for h in range(ATT_HEADS):
            g, par, pair = h // ATT_GROUP, h % 2, h // 2
            qpair = q16[r0:r0 + ATT_QBLK, 2 * ATT_HEAD_DIM * pair:2 * ATT_HEAD_DIM * (pair + 1)]
            s = _dot_nt(qpair, kz[g][par][r0:r0 + ATT_KBLK])
            s = jnp.where(allowed,
                          s * (ATT_HEAD_DIM ** -0.5) - (2.0 ** -(h + 1)) * absdist, -jnp.inf)
            sink = sinks_ref[h]
            m = jnp.maximum(jnp.max(s, axis=-1, keepdims=True), sink)
            p = jnp.exp(s - m)
            inv_den[blk][h] = 1.0 / (jnp.sum(p, axis=-1, keepdims=True) + jnp.exp(sink - m))
            probs[blk][h] = p.astype(_BF16)

    proj_ref[:, _OFF_GA:IN_W] = _dot(n_ref[...], win_ref[:, _OFF_GA:IN_W])

    lraw = lbraw_ref[...]
    lexp = jnp.exp(lraw - jnp.max(lraw, axis=0, keepdims=True))
    lb = lexp[0:1, :] / jnp.sum(lexp, axis=0, keepdims=True)
    ri = lax.broadcasted_iota(jnp.int32, (HG_BLK, HG_BLK), 0)
    ci = lax.broadcasted_iota(jnp.int32, (HG_BLK, HG_BLK), 1)
    chunk_causal = (lax.shift_right_logical(ri, 6) == lax.shift_right_logical(ci, 6)) & (ri >= ci)
    rowchunk = lax.shift_right_logical(lax.broadcasted_iota(jnp.int32, (HG_BLK, HG_DIM), 0), 6)
    zero_hd = jnp.zeros((HG_BLK, HG_DIM), _BF16)
    ncb = HG_BLK // CHUNK
    hg_rows = [slice(r0, r0 + HG_BLK) for r0 in range(0, tt, HG_BLK)]

    def chunk_blocks(a):
        return jnp.concatenate([jnp.where(rowchunk == c, a, zero_hd) for c in range(ncb)], axis=1)

    def decayed_operands(rows):
        f = lb + (1.0 - lb) * jax.nn.sigmoid(proj_ref[rows, _OFF_HF:_OFF_HF + HG_W])
        b = _chunk_cumsum(jnp.log(f))
        b_last = [b[(c + 1) * CHUNK - 1:(c + 1) * CHUNK, :] for c in range(ncb)]
        b_last_rows = jnp.concatenate(
            [jnp.broadcast_to(bl, (CHUNK, HG_W)) for bl in b_last], axis=0)
        q_dec = (proj_ref[rows, _OFF_HQ:_OFF_HQ + HG_W] * (HG_DIM ** -0.5)
                 * jnp.exp(b)).astype(_BF16)
        k_inv = ((1.0 - f) * jnp.exp(-b)).astype(_BF16)
        k_end = ((1.0 - f) * jnp.exp(b_last_rows - b)).astype(_BF16)
        return q_dec, k_inv, k_end, [jnp.exp(bl) for bl in b_last]

    operands = [decayed_operands(rows) for rows in hg_rows]

    olane = lax.broadcasted_iota(jnp.int32, (ATT_QBLK, 2 * ATT_HEAD_DIM), 1)
    for blk in range(nblk):
        r0 = blk * ATT_QBLK
        for pair in range(ATT_HEADS // 2):
            g, he, ho = (2 * pair) // ATT_GROUP, 2 * pair, 2 * pair + 1
            o2 = (_dot(probs[blk][he], vz[g][0][r0:r0 + ATT_KBLK])
                  + _dot(probs[blk][ho], vz[g][1][r0:r0 + ATT_KBLK]))
            scale2 = jnp.where(olane < ATT_HEAD_DIM, inv_den[blk][he], inv_den[blk][ho])
            ao_ref[r0:r0 + ATT_QBLK, 2 * ATT_HEAD_DIM * pair:2 * ATT_HEAD_DIM * (pair + 1)] = (
                (o2 * scale2).astype(_BF16))

    def state_free_dots(rows, q_dec, k_inv, k_end):
        a_mats, ds_all, v16 = [], [], []
        for hh in range(HG_HEADS):
            sl = slice(hh * HG_DIM, (hh + 1) * HG_DIM)
            a_mats.append(
                jnp.where(chunk_causal, _dot_nt(q_dec[:, sl], k_inv[:, sl]), 0.0).astype(_BF16))
            v = proj_ref[rows, _OFF_HI + hh * HG_DIM:_OFF_HI + (hh + 1) * HG_DIM]
            v16.append(v.astype(_BF16))
            ds_all.append(_dot(v.T.astype(_BF16), chunk_blocks(k_end[:, sl])))
        return a_mats, ds_all, v16

    local = [state_free_dots(rows, *ops[:3]) for rows, ops in zip(hg_rows, operands)]

    ya = _dot(ao_ref[...], wa_ref[...])

    gon = gon_ref[...]
    state = [st_ref[hh] for hh in range(HG_HEADS)]
    for rows, (q_dec, _, _, decay), (a_mats, ds_all, v16) in zip(hg_rows, operands, local):
        for hh in range(HG_HEADS):
            sl = slice(hh * HG_DIM, (hh + 1) * HG_DIM)
            st = state[hh]
            entering = []
            for c in range(ncb):
                entering.append(st.astype(_BF16))
                st = st * decay[c][:, sl] + ds_all[hh][:, c * HG_DIM:(c + 1) * HG_DIM]
            state[hh] = st
            o = (_dot(a_mats[hh], v16[hh])
                 + _dot_nt(chunk_blocks(q_dec[:, sl]), jnp.concatenate(entering, axis=1)))
            gate = jax.nn.silu(proj_ref[rows, _OFF_HG + hh * HG_DIM:_OFF_HG + (hh + 1) * HG_DIM])
            y = o * lax.rsqrt(jnp.mean(o * o, axis=-1, keepdims=True) + EPS) * gon * gate
            hy_ref[rows, sl] = y.astype(_BF16)
    for hh in range(HG_HEADS):
        st_ref[hh] = state[hh]

    yb = _dot(hy_ref[...], wb_ref[...])
    y = (jax.nn.sigmoid(proj_ref[:, _OFF_GA:_OFF_GA + D_MODEL]) * ya
         + jax.nn.sigmoid(proj_ref[:, _OFF_GB:_OFF_GB + D_MODEL]) * yb)
    o_ref[...] = x_ref[...] + _dot(y.astype(_BF16), wmix_ref[...])

    for ref in (kbuf_ref, kswp_ref, vbuf_ref, vswp_ref):
        ref[0:KV_HALO, :] = ref[tt:tt + KV_HALO, :]
    _cast_slabs(later_f32, later_bf16)


def _const_spec(shape):
    return pl.BlockSpec(shape, lambda b, t: (0,) * len(shape), pipeline_mode=pl.Buffered(1))


def _slab_spec(shape, nsteps, linear_step):
    rows, cols = shape
    slab = next(s for s in range(BF16_ROWS, rows + 1, BF16_ROWS)
                if rows % s == 0 and rows // s <= nsteps)
    nslabs = rows // slab
    return pl.BlockSpec((slab, cols), lambda *ids: (linear_step(*ids) * nslabs // nsteps, 0))


def _cast_slabs(later_f32, later_bf16):
    for src, dst in zip(later_f32, later_bf16):
        dst[...] = src[...].astype(_BF16)


def _mixer(x, g_mix, w_in, lb_raw, sinks, g_onorm, w_a, w_b, w_mix, later_weights, tt):
    bsz, seq, d = x.shape
    nt = seq // tt
    tile = pl.BlockSpec((None, tt, d), lambda b, t: (b, t, 0))
    slabs = [_slab_spec(w.shape, bsz * nt, lambda b, t: b * nt + t) for w in later_weights]
    h, *later16 = pl.pallas_call(
        functools.partial(_mixer_kernel, tt=tt, n_later=len(later_weights)),
        grid=(bsz, nt),
        in_specs=[
            tile,
            _const_spec(g_mix.shape),
            _const_spec(w_in.shape),
            _const_spec(lb_raw.shape),
            pl.BlockSpec(memory_space=pltpu.SMEM),
            _const_spec(g_onorm.shape),
            _const_spec(w_a.shape),
            _const_spec(w_b.shape),
            _const_spec(w_mix.shape),
        ] + slabs,
        out_specs=[tile] + slabs,
        out_shape=[jax.ShapeDtypeStruct(x.shape, _F32)]
        + [jax.ShapeDtypeStruct(w.shape, _BF16) for w in later_weights],
        scratch_shapes=[
            pltpu.VMEM((tt, d), _BF16),
            pltpu.VMEM((tt, IN_W), _F32),
            pltpu.VMEM((KV_HALO + tt, ATT_KV_W), _BF16),
            pltpu.VMEM((KV_HALO + tt, ATT_KV_W), _BF16),
            pltpu.VMEM((KV_HALO + tt, ATT_KV_W), _BF16),
            pltpu.VMEM((KV_HALO + tt, ATT_KV_W), _BF16),
            pltpu.VMEM((HG_HEADS, HG_DIM, HG_DIM), _F32),
            pltpu.VMEM((tt, ATT_Q_W), _BF16),
            pltpu.VMEM((tt, HG_W), _BF16),
        ],
        compiler_params=pltpu.CompilerParams(
            dimension_semantics=("arbitrary", "arbitrary"),
            vmem_limit_bytes=VMEM_LIMIT_BYTES),
        name="mixer",
    )(x, g_mix, w_in, lb_raw, sinks, g_onorm, w_a, w_b, w_mix, *later_weights)
    return h, later16


def _mem_qk_vo_kernel(mem_ref, gmem_ref, wkv_ref, wq_ref, wo_ref, *rest, n_later, m):
    later_f32 = rest[:n_later]
    qk_ref, vo_ref = rest[n_later:n_later + 2]
    later_bf16 = rest[n_later + 2:2 * n_later + 2]
    wkv16_ref, wq16_ref, wo16_ref = rest[2 * n_later + 2:]

    @pl.when(pl.program_id(0) == 0)
    def _():
        wkv16_ref[...] = wkv_ref[...].astype(_BF16)
        wq16_ref[...] = wq_ref[...].astype(_BF16)
        wo16_ref[...] = wo_ref[...].astype(_BF16)

    nm = _rms(mem_ref[...], gmem_ref[...]).astype(_BF16)
    kv = _dot(nm, wkv16_ref[...]).astype(_BF16)
    for hd in range(X_HEADS):
        dims = slice(hd * X_HEAD_DIM, (hd + 1) * X_HEAD_DIM)
        vdims = slice(D_MODEL + hd * X_HEAD_DIM, D_MODEL + (hd + 1) * X_HEAD_DIM)
        mems = slice(hd * m, (hd + 1) * m)
        qk_ref[:, mems] = _dot_nt(wq16_ref[:, dims], kv[:, dims]).astype(_BF16)
        vo_ref[mems, :] = _dot(kv[:, vdims], wo16_ref[dims, :]).astype(_BF16)
    _cast_slabs(later_f32, later_bf16)


def _mem_qk_vo(mem, g_mem, w_ckv, w_cq, w_co, later_weights):
    bsz, m, d = mem.shape
    slabs = [_slab_spec(w.shape, bsz, lambda b: b) for w in later_weights]
    resident = lambda w: pl.BlockSpec(w.shape, lambda b: (0, 0), pipeline_mode=pl.Buffered(1))
    qk, vo, *later16 = pl.pallas_call(
        functools.partial(_mem_qk_vo_kernel, n_later=len(later_weights), m=m),
        grid=(bsz,),
        in_specs=[
            pl.BlockSpec((None, m, d), lambda b: (b, 0, 0)),
            pl.BlockSpec(g_mem.shape, lambda b: (0, 0)),
            resident(w_ckv), resident(w_cq), resident(w_co),
        ] + slabs,
        out_specs=[pl.BlockSpec((None, d, X_HEADS * m), lambda b: (b, 0, 0)),
                   pl.BlockSpec((None, X_HEADS * m, d), lambda b: (b, 0, 0))] + slabs,
        out_shape=[jax.ShapeDtypeStruct((bsz, d, X_HEADS * m), _BF16),
                   jax.ShapeDtypeStruct((bsz, X_HEADS * m, d), _BF16)]
        + [jax.ShapeDtypeStruct(w.shape, _BF16) for w in later_weights],
        scratch_shapes=[pltpu.VMEM(w.shape, _BF16) for w in (w_ckv, w_cq, w_co)],
        compiler_params=pltpu.CompilerParams(
            dimension_semantics=("arbitrary",), vmem_limit_bytes=VMEM_LIMIT_BYTES),
        name="mem_qk_vo",
    )(mem, g_mem, w_ckv, w_cq, w_co, *later_weights)
    return qk, vo, later16


def _cross_kernel(h_ref, gc_ref, qk_ref, vo_ref, o_ref, *, m):
    h = h_ref[...]
    s_all = _dot(_rms(h, gc_ref[...]).astype(_BF16), qk_ref[...]) * (X_HEAD_DIM ** -0.5)
    probs = []
    for hd in range(X_HEADS):
        s = s_all[:, hd * m:(hd + 1) * m]
        p = jnp.exp(s - jnp.max(s, axis=-1, keepdims=True))
        probs.append((p * (1.0 / jnp.sum(p, axis=-1, keepdims=True))).astype(_BF16))
    o_ref[...] = h + _dot(jnp.concatenate(probs, axis=-1), vo_ref[...])


def _cross(h, g_cross, qk, vo, tt):
    bsz, seq, d = h.shape
    m = vo.shape[1] // X_HEADS
    tile = pl.BlockSpec((None, tt, d), lambda b, t: (b, t, 0))
    return pl.pallas_call(
        functools.partial(_cross_kernel, m=m),
        grid=(bsz, seq // tt),
        in_specs=[
            tile,
            _const_spec(g_cross.shape),
            pl.BlockSpec((None,) + qk.shape[1:], lambda b, t: (b, 0, 0)),
            pl.BlockSpec((None,) + vo.shape[1:], lambda b, t: (b, 0, 0)),
        ],
        out_specs=tile,
        out_shape=jax.ShapeDtypeStruct(h.shape, _F32),
        compiler_params=pltpu.CompilerParams(
            dimension_semantics=("arbitrary", "arbitrary"),
            vmem_limit_bytes=VMEM_LIMIT_BYTES),
        name="cross",
    )(h, g_cross, qk, vo)


def _shift_rows(a, first_row):
    rolled = pltpu.roll(a, 1, 0)
    row = lax.broadcasted_iota(jnp.int32, (SUBLANES, a.shape[1]), 0)
    head = jnp.where(row == 0, first_row, rolled[0:SUBLANES])
    return jnp.concatenate([head, rolled[SUBLANES:]], axis=0)


def _ffn_kernel(h_ref, gf_ref, win_ref, cw_ref, cb_ref, wd_ref, gfin_ref, o_ref,
                tail_ref, n_ref, act_ref, *, tt):
    t = pl.program_id(1)

    @pl.when(t == 0)
    def _():
        tail_ref[...] = jnp.zeros(tail_ref.shape, _F32)

    n_ref[...] = _rms(h_ref[...], gf_ref[...]).astype(_BF16)
    for c0 in range(0, D_FF, FF_COLS):
        cols = slice(c0, c0 + FF_COLS)
        u = _dot(n_ref[...], win_ref[:, cols])
        gate = _dot(n_ref[...], win_ref[:, D_FF + c0:D_FF + c0 + FF_COLS])
        w0, w1, w2 = cw_ref[0:1, cols], cw_ref[1:2, cols], cw_ref[2:3, cols]
        prev1 = tail_ref[SUBLANES - 1:SUBLANES, cols]
        prev2 = tail_ref[SUBLANES - 2:SUBLANES - 1, cols]
        acc = w1 * u + _shift_rows(w0 * u, w0 * prev1)
        conv = w2 * u + _shift_rows(acc, w1 * prev1 + w0 * prev2) + cb_ref[:, cols]
        tail_ref[:, cols] = u[tt - SUBLANES:tt, :]
        act_ref[:, cols] = (jax.nn.silu(conv) * gate).astype(_BF16)
    for r0 in range(0, tt, FF_OUT_ROWS):
        rows = slice(r0, r0 + FF_OUT_ROWS)
        h3 = h_ref[rows, :] + _dot(act_ref[rows, :], wd_ref[...])
        o_ref[rows, :] = _rms(h3, gfin_ref[...])


def _ffn(h, g_ffn, w_ffn_in, conv_w, conv_b, w_down, g_final, tt):
    bsz, seq, d = h.shape
    tile = pl.BlockSpec((None, tt, d), lambda b, t: (b, t, 0))
    return pl.pallas_call(
        functools.partial(_ffn_kernel, tt=tt),
        grid=(bsz, seq // tt),
        in_specs=[
            tile,
            _const_spec(g_ffn.shape),
            _const_spec(w_ffn_in.shape),
            _const_spec(conv_w.shape),
            _const_spec(conv_b.shape),
            _const_spec(w_down.shape),
            _const_spec(g_final.shape),
        ],
        out_specs=tile,
        out_shape=jax.ShapeDtypeStruct(h.shape, _F32),
        scratch_shapes=[
            pltpu.VMEM((SUBLANES, D_FF), _F32),
            pltpu.VMEM((tt, d), _BF16),
            pltpu.VMEM((tt, D_FF), _BF16),
        ],
        compiler_params=pltpu.CompilerParams(
            dimension_semantics=("arbitrary", "arbitrary"),
            vmem_limit_bytes=VMEM_LIMIT_BYTES),
        name="ffn",
    )(h, g_ffn, w_ffn_in, conv_w, conv_b, w_down, g_final)


def kernel(x, mem, g_mix, w_in, lower_bounds, attn_sinks, g_onorm, w_branch_a, w_branch_b,
           w_mix_out, g_cross, g_mem, w_cq, w_ckv, w_co, g_ffn, w_ffn_in, conv_w, conv_b,
           w_ffn_down, g_final):
    depth = g_mix.shape[0]
    assert depth == 1 and x.shape[-1] == D_MODEL and x.shape[1] % TOKEN_TILE == 0
    tt = TOKEN_TILE
    h = x
    for l in range(depth):
        qk, vo, (in16, a16, b16, mix16) = _mem_qk_vo(
            mem, g_mem[l][None], w_ckv[l], w_cq[l], w_co[l],
            [w_in[l], w_branch_a[l], w_branch_b[l], w_mix_out[l]])
        h, (ffn_in16, ffn_down16) = _mixer(
            h, g_mix[l][None], in16, lower_bounds.astype(_F32), attn_sinks[l], g_onorm[l][None],
            a16, b16, mix16, [w_ffn_in[l], w_ffn_down[l]], tt)
        h = _cross(h, g_cross[l][None], qk, vo, CROSS_TILE)
        h = _ffn(h, g_ffn[l][None], ffn_in16, conv_w[l], conv_b[l][None], ffn_down16,
                 g_final[None], FFN_TILE)
    return h
```

```python
import functools

import jax
import jax.numpy as jnp
from jax import lax
from jax.experimental import pallas as pl
from jax.experimental.pallas import tpu as pltpu

D_MODEL = 1024
CHUNK = 64
EPS = 1e-6

ATT_HEADS = 8
ATT_KV_HEADS = 2
ATT_HEAD_DIM = 64
ATT_GROUP = ATT_HEADS // ATT_KV_HEADS
LOOKBACK = 2
ATT_Q_W = ATT_HEADS * ATT_HEAD_DIM
ATT_KV_W = ATT_KV_HEADS * ATT_HEAD_DIM
KV_HALO = LOOKBACK * CHUNK
ATT_QBLK = 2 * CHUNK
ATT_KBLK = ATT_QBLK + KV_HALO

HG_HEADS = 4
HG_DIM = 128
HG_W = HG_HEADS * HG_DIM
HG_BLK = 4 * CHUNK

X_HEADS = 4
X_HEAD_DIM = D_MODEL // X_HEADS

D_FF = 2816
CONV_WIDTH = 3
FF_COLS = 256
FF_OUT_ROWS = 256
SUBLANES = 8
BF16_ROWS = 16

_OFF_AQ = 0
_OFF_AK = _OFF_AQ + ATT_Q_W
_OFF_AV = _OFF_AK + ATT_KV_W
_OFF_HQ = _OFF_AV + ATT_KV_W
_OFF_HF = _OFF_HQ + HG_W
_OFF_HI = _OFF_HF + HG_W
_OFF_HG = _OFF_HI + HG_W
_OFF_GA = _OFF_HG + HG_W
_OFF_GB = _OFF_GA + D_MODEL
IN_W = _OFF_GB + D_MODEL

TOKEN_TILE = 512
CROSS_TILE = 1024
FFN_TILE = 1024
VMEM_LIMIT_BYTES = 56 * 1024 * 1024

_F32 = jnp.float32
_BF16 = jnp.bfloat16
_NT = (((1,), (1,)), ((), ()))


def _rms(xf, gain):
    return xf * lax.rsqrt(jnp.mean(xf * xf, axis=-1, keepdims=True) + EPS) * gain


def _dot(a, b):
    return jnp.dot(a, b, preferred_element_type=_F32)


def _dot_nt(a, b):
    return lax.dot_general(a, b, _NT, preferred_element_type=_F32)


def _chunk_cumsum(a):
    rows, cols = a.shape
    sub = lax.broadcasted_iota(jnp.int32, (SUBLANES, cols), 0)
    groups = []
    for g0 in range(0, rows, SUBLANES):
        g = a[g0:g0 + SUBLANES, :]
        step = 1
        while step < SUBLANES:
            g = g + jnp.where(sub >= step, pltpu.roll(g, step, 0), 0.0)
            step *= 2
        if g0 % CHUNK:
            g = g + carry
        carry = jnp.broadcast_to(g[SUBLANES - 1:SUBLANES, :], (SUBLANES, cols))
        groups.append(g)
    return jnp.concatenate(groups, axis=0)


def _mixer_kernel(x_ref, gmix_ref, win_ref, lbraw_ref, sinks_ref, gon_ref, wa_ref, wb_ref,
                  wmix_ref, *rest, tt, n_later):
    later_f32 = rest[:n_later]
    o_ref = rest[n_later]
    later_bf16 = rest[n_later + 1:2 * n_later + 1]
    (n_ref, proj_ref, kbuf_ref, kswp_ref, vbuf_ref, vswp_ref,
     st_ref, ao_ref, hy_ref) = rest[2 * n_later + 1:]
    t = pl.program_id(1)
    win = KV_HALO + tt

    @pl.when(t == 0)
    def _():
        for ref in (kbuf_ref, kswp_ref, vbuf_ref, vswp_ref):
            ref[0:KV_HALO, :] = jnp.zeros((KV_HALO, ATT_KV_W), _BF16)
        st_ref[...] = jnp.zeros(st_ref.shape, _F32)

    n_ref[...] = _rms(x_ref[...], gmix_ref[...]).astype(_BF16)
    proj_ref[:, 0:_OFF_HQ] = _dot(n_ref[...], win_ref[:, 0:_OFF_HQ])
    proj_ref[:, _OFF_HQ:_OFF_GA] = _dot(n_ref[...], win_ref[:, _OFF_HQ:_OFF_GA])

    new = slice(KV_HALO, win)
    k32 = proj_ref[:, _OFF_AK:_OFF_AK + ATT_KV_W]
    v32 = proj_ref[:, _OFF_AV:_OFF_AV + ATT_KV_W]
    kbuf_ref[new, :] = k32.astype(_BF16)
    vbuf_ref[new, :] = v32.astype(_BF16)
    kswp_ref[new, :] = pltpu.roll(k32, ATT_HEAD_DIM, 1).astype(_BF16)
    vswp_ref[new, :] = pltpu.roll(v32, ATT_HEAD_DIM, 1).astype(_BF16)

    kvlane = lax.broadcasted_iota(jnp.int32, (win, ATT_KV_W), 1)
    halves = (kvlane < ATT_HEAD_DIM, kvlane >= ATT_HEAD_DIM)
    zero_kv = jnp.zeros((win, ATT_KV_W), _BF16)

    def half_placed(nat_ref, swp_ref, g, par):
        src = nat_ref if g == par else swp_ref
        return jnp.where(halves[par], src[...], zero_kv)

    kz = [[half_placed(kbuf_ref, kswp_ref, g, par) for par in range(2)]
          for g in range(ATT_KV_HEADS)]
    vz = [[half_placed(vbuf_ref, vswp_ref, g, par) for par in range(2)]
          for g in range(ATT_KV_HEADS)]
    qi = lax.broadcasted_iota(jnp.int32, (ATT_QBLK, ATT_KBLK), 0)
    kj = lax.broadcasted_iota(jnp.int32, (ATT_QBLK, ATT_KBLK), 1)
    qc = lax.shift_right_logical(qi, 6)
    kc = lax.shift_right_logical(kj, 6)
    band = (kc >= qc) & (kc <= qc + LOOKBACK)
    absdist = jnp.abs(qi + KV_HALO - kj).astype(_F32)
    alibi = [jnp.where(band, -(2.0 ** -(h + 1)) * absdist, -jnp.inf) for h in range(ATT_HEADS)]
    q16 = (proj_ref[:, _OFF_AQ:_OFF_AQ + ATT_Q_W] * (ATT_HEAD_DIM ** -0.5)).astype(_BF16)
    nblk = tt // ATT_QBLK
    probs = [[None] * ATT_HEADS for _ in range(nblk)]
    inv_den = [[None] * ATT_HEADS for _ in range(nblk)]
    for blk in range(nblk):
        r0 = blk * ATT_QBLK
        bias = ([jnp.where((t * tt + kj) >= KV_HALO, a, -jnp.inf) for a in alibi]
                if blk == 0 else alibi)
        for g in range(ATT_KV_HEADS):
            q2 = jnp.concatenate(
                [q16[r0:r0 + ATT_QBLK, 2 * ATT_HEAD_DIM * pair:2 * ATT_HEAD_DIM * (pair + 1)]
                 for pair in (2 * g, 2 * g + 1)], axis=0)
            for par in range(2):
                s2 = _dot_nt(q2, kz[g][par][r0:r0 + ATT_KBLK])
                for half, h in enumerate((ATT_GROUP * g + par, ATT_GROUP * g + par + 2)):
                    s = s2[half * ATT_QBLK:(half + 1) * ATT_QBLK] + bias[h]
                    sink = sinks_ref[h]
                    m = jnp.maximum(jnp.max(s, axis=-1, keepdims=True), sink)
                    p = jnp.exp(s - m)
                    inv_den[blk][h] = 1.0 / (
                        jnp.sum(p, axis=-1, keepdims=True) + jnp.exp(sink - m))
                    probs[blk][h] = p.astype(_BF16)

    proj_ref[:, _OFF_GA:IN_W] = _dot(n_ref[...], win_ref[:, _OFF_GA:IN_W])

    lraw = lbraw_ref[...]
    lexp = jnp.exp(lraw - jnp.max(lraw, axis=0, keepdims=True))
    lb = lexp[0:1, :] / jnp.sum(lexp, axis=0, keepdims=True)
    ri = lax.broadcasted_iota(jnp.int32, (HG_BLK, HG_BLK), 0)
    ci = lax.broadcasted_iota(jnp.int32, (HG_BLK, HG_BLK), 1)
    chunk_causal = (lax.shift_right_logical(ri, 6) == lax.shift_right_logical(ci, 6)) & (ri >= ci)
    rowchunk = lax.shift_right_logical(lax.broadcasted_iota(jnp.int32, (HG_BLK, HG_DIM), 0), 6)
    zero_hd = jnp.zeros((HG_BLK, HG_DIM), _BF16)
    ncb = HG_BLK // CHUNK
    hg_rows = [slice(r0, r0 + HG_BLK) for r0 in range(0, tt, HG_BLK)]

    def chunk_blocks(a):
        return jnp.concatenate([jnp.where(rowchunk == c, a, zero_hd) for c in range(ncb)], axis=1)

    def decayed_operands(rows):
        f = lb + (1.0 - lb) * jax.nn.sigmoid(proj_ref[rows, _OFF_HF:_OFF_HF + HG_W])
        b = _chunk_cumsum(jnp.log(f))
        b_last = [b[(c + 1) * CHUNK - 1:(c + 1) * CHUNK, :] for c in range(ncb)]
        b_last_rows = jnp.concatenate(
            [jnp.broadcast_to(bl, (CHUNK, HG_W)) for bl in b_last], axis=0)
        q_dec = (proj_ref[rows, _OFF_HQ:_OFF_HQ + HG_W] * (HG_DIM ** -0.5)
                 * jnp.exp(b)).astype(_BF16)
        k_inv = ((1.0 - f) * jnp.exp(-b)).astype(_BF16)
        k_end = ((1.0 - f) * jnp.exp(b_last_rows - b)).astype(_BF16)
        return q_dec, k_inv, k_end, [jnp.exp(bl) for bl in b_last]

    operands = [decayed_operands(rows) for rows in hg_rows]

    olane = lax.broadcasted_iota(jnp.int32, (ATT_QBLK, 2 * ATT_HEAD_DIM), 1)
    for blk in range(nblk):
        r0 = blk * ATT_QBLK
        for g in range(ATT_KV_HEADS):
            h0 = ATT_GROUP * g
            o4 = (_dot(jnp.concatenate([probs[blk][h0], probs[blk][h0 + 2]], axis=0),
                       vz[g][0][r0:r0 + ATT_KBLK])
                  + _dot(jnp.concatenate([probs[blk][h0 + 1], probs[blk][h0 + 3]], axis=0),
                         vz[g][1][r0:r0 + ATT_KBLK]))
            for half in range(2):
                he, pair = h0 + 2 * half, 2 * g + half
                scale2 = jnp.where(olane < ATT_HEAD_DIM, inv_den[blk][he], inv_den[blk][he + 1])
                o2 = o4[half * ATT_QBLK:(half + 1) * ATT_QBLK]
                ao_ref[r0:r0 + ATT_QBLK, 2 * ATT_HEAD_DIM * pair:2 * ATT_HEAD_DIM * (pair + 1)] = (
                    (o2 * scale2).astype(_BF16))

    def state_free_dots(rows, q_dec, k_inv, k_end):
        a_mats, ds_all, v16 = [], [], []
        for hh in range(HG_HEADS):
            sl = slice(hh * HG_DIM, (hh + 1) * HG_DIM)
            a_mats.append(
                jnp.where(chunk_causal, _dot_nt(q_dec[:, sl], k_inv[:, sl]), 0.0).astype(_BF16))
            v = proj_ref[rows, _OFF_HI + hh * HG_DIM:_OFF_HI + (hh + 1) * HG_DIM]
            v16.append(v.astype(_BF16))
            ds_all.append(_dot(v.T.astype(_BF16), chunk_blocks(k_end[:, sl])))
        return a_mats, ds_all, v16

    local = [state_free_dots(rows, *ops[:3]) for rows, ops in zip(hg_rows, operands)]

    ya = _dot(ao_ref[...], wa_ref[...])

    gon = gon_ref[...]
    state = [st_ref[hh] for hh in range(HG_HEADS)]
    for rows, (q_dec, _, _, decay), (a_mats, ds_all, v16) in zip(hg_rows, operands, local):
        for hh in range(HG_HEADS):
            sl = slice(hh * HG_DIM, (hh + 1) * HG_DIM)
            st = state[hh]
            entering = []
            for c in range(ncb):
                entering.append(st.astype(_BF16))
                st = st * decay[c][:, sl] + ds_all[hh][:, c * HG_DIM:(c + 1) * HG_DIM]
            state[hh] = st
            o = (_dot(a_mats[hh], v16[hh])
                 + _dot_nt(chunk_blocks(q_dec[:, sl]), jnp.concatenate(entering, axis=1)))
            gate = jax.nn.silu(proj_ref[rows, _OFF_HG + hh * HG_DIM:_OFF_HG + (hh + 1) * HG_DIM])
            y = o * lax.rsqrt(jnp.mean(o * o, axis=-1, keepdims=True) + EPS) * gon * gate
            hy_ref[rows, sl] = y.astype(_BF16)
    for hh in range(HG_HEADS):
        st_ref[hh] = state[hh]

    yb = _dot(hy_ref[...], wb_ref[...])
    y = (jax.nn.sigmoid(proj_ref[:, _OFF_GA:_OFF_GA + D_MODEL]) * ya
         + jax.nn.sigmoid(proj_ref[:, _OFF_GB:_OFF_GB + D_MODEL]) * yb)
    o_ref[...] = x_ref[...] + _dot(y.astype(_BF16), wmix_ref[...])

    for ref in (kbuf_ref, kswp_ref, vbuf_ref, vswp_ref):
        ref[0:KV_HALO, :] = ref[tt:tt + KV_HALO, :]
    _cast_slabs(later_f32, later_bf16)


def _const_spec(shape):
    return pl.BlockSpec(shape, lambda b, t: (0,) * len(shape), pipeline_mode=pl.Buffered(1))


def _slab_spec(shape, nsteps, linear_step):
    rows, cols = shape
    slab = next(s for s in range(BF16_ROWS, rows + 1, BF16_ROWS)
                if rows % s == 0 and rows // s <= nsteps)
    nslabs = rows // slab
    return pl.BlockSpec((slab, cols), lambda *ids: (linear_step(*ids) * nslabs // nsteps, 0))


def _cast_slabs(later_f32, later_bf16):
    for src, dst in zip(later_f32, later_bf16):
        dst[...] = src[...].astype(_BF16)


def _mixer(x, g_mix, w_in, lb_raw, sinks, g_onorm, w_a, w_b, w_mix, later_weights, tt):
    bsz, seq, d = x.shape
    nt = seq // tt
    tile = pl.BlockSpec((None, tt, d), lambda b, t: (b, t, 0))
    slabs = [_slab_spec(w.shape, bsz * nt, lambda b, t: b * nt + t) for w in later_weights]
    h, *later16 = pl.pallas_call(
        functools.partial(_mixer_kernel, tt=tt, n_later=len(later_weights)),
        grid=(bsz, nt),
        in_specs=[
            tile,
            _const_spec(g_mix.shape),
            _const_spec(w_in.shape),
            _const_spec(lb_raw.shape),
            pl.BlockSpec(memory_space=pltpu.SMEM),
            _const_spec(g_onorm.shape),
            _const_spec(w_a.shape),
            _const_spec(w_b.shape),
            _const_spec(w_mix.shape),
        ] + slabs,
        out_specs=[tile] + slabs,
        out_shape=[jax.ShapeDtypeStruct(x.shape, _F32)]
        + [jax.ShapeDtypeStruct(w.shape, _BF16) for w in later_weights],
        scratch_shapes=[
            pltpu.VMEM((tt, d), _BF16),
            pltpu.VMEM((tt, IN_W), _F32),
            pltpu.VMEM((KV_HALO + tt, ATT_KV_W), _BF16),
            pltpu.VMEM((KV_HALO + tt, ATT_KV_W), _BF16),
            pltpu.VMEM((KV_HALO + tt, ATT_KV_W), _BF16),
            pltpu.VMEM((KV_HALO + tt, ATT_KV_W), _BF16),
            pltpu.VMEM((HG_HEADS, HG_DIM, HG_DIM), _F32),
            pltpu.VMEM((tt, ATT_Q_W), _BF16),
            pltpu.VMEM((tt, HG_W), _BF16),
        ],
        compiler_params=pltpu.CompilerParams(
            dimension_semantics=("arbitrary", "arbitrary"),
            vmem_limit_bytes=VMEM_LIMIT_BYTES),
        name="mixer",
    )(x, g_mix, w_in, lb_raw, sinks, g_onorm, w_a, w_b, w_mix, *later_weights)
    return h, later16


def _mem_qk_vo_kernel(mem_ref, gmem_ref, wkv_ref, wq_ref, wo_ref, *rest, n_later, m):
    later_f32 = rest[:n_later]
    qk_ref, vo_ref = rest[n_later:n_later + 2]
    later_bf16 = rest[n_later + 2:2 * n_later + 2]
    wkv16_ref, wq16_ref, wo16_ref = rest[2 * n_later + 2:]

    @pl.when(pl.program_id(0) == 0)
    def _():
        wkv16_ref[...] = wkv_ref[...].astype(_BF16)
        wq16_ref[...] = wq_ref[...].astype(_BF16)
        wo16_ref[...] = wo_ref[...].astype(_BF16)

    nm = _rms(mem_ref[...], gmem_ref[...]).astype(_BF16)
    kv = _dot(nm, wkv16_ref[...]).astype(_BF16)
    for hd in range(X_HEADS):
        dims = slice(hd * X_HEAD_DIM, (hd + 1) * X_HEAD_DIM)
        vdims = slice(D_MODEL + hd * X_HEAD_DIM, D_MODEL + (hd + 1) * X_HEAD_DIM)
        mems = slice(hd * m, (hd + 1) * m)
        qk_ref[:, mems] = _dot_nt(wq16_ref[:, dims], kv[:, dims]).astype(_BF16)
        vo_ref[mems, :] = _dot(kv[:, vdims], wo16_ref[dims, :]).astype(_BF16)
    _cast_slabs(later_f32, later_bf16)


def _mem_qk_vo(mem, g_mem, w_ckv, w_cq, w_co, later_weights):
    bsz, m, d = mem.shape
    slabs = [_slab_spec(w.shape, bsz, lambda b: b) for w in later_weights]
    resident = lambda w: pl.BlockSpec(w.shape, lambda b: (0, 0), pipeline_mode=pl.Buffered(1))
    qk, vo, *later16 = pl.pallas_call(
        functools.partial(_mem_qk_vo_kernel, n_later=len(later_weights), m=m),
        grid=(bsz,),
        in_specs=[
            pl.BlockSpec((None, m, d), lambda b: (b, 0, 0)),
            pl.BlockSpec(g_mem.shape, lambda b: (0, 0)),
            resident(w_ckv), resident(w_cq), resident(w_co),
        ] + slabs,
        out_specs=[pl.BlockSpec((None, d, X_HEADS * m), lambda b: (b, 0, 0)),
                   pl.BlockSpec((None, X_HEADS * m, d), lambda b: (b, 0, 0))] + slabs,
        out_shape=[jax.ShapeDtypeStruct((bsz, d, X_HEADS * m), _BF16),
                   jax.ShapeDtypeStruct((bsz, X_HEADS * m, d), _BF16)]
        + [jax.ShapeDtypeStruct(w.shape, _BF16) for w in later_weights],
        scratch_shapes=[pltpu.VMEM(w.shape, _BF16) for w in (w_ckv, w_cq, w_co)],
        compiler_params=pltpu.CompilerParams(
            dimension_semantics=("arbitrary",), vmem_limit_bytes=VMEM_LIMIT_BYTES),
        name="mem_qk_vo",
    )(mem, g_mem, w_ckv, w_cq, w_co, *later_weights)
    return qk, vo, later16


def _cross_kernel(h_ref, gc_ref, qk_ref, vo_ref, o_ref, *, m):
    h = h_ref[...]
    s_all = _dot(_rms(h, gc_ref[...]).astype(_BF16), qk_ref[...]) * (X_HEAD_DIM ** -0.5)
    probs = []
    for hd in range(X_HEADS):
        s = s_all[:, hd * m:(hd + 1) * m]
        p = jnp.exp(s - jnp.max(s, axis=-1, keepdims=True))
        probs.append((p * (1.0 / jnp.sum(p, axis=-1, keepdims=True))).astype(_BF16))
    o_ref[...] = h + _dot(jnp.concatenate(probs, axis=-1), vo_ref[...])


def _cross(h, g_cross, qk, vo, tt):
    bsz, seq, d = h.shape
    m = vo.shape[1] // X_HEADS
    tile = pl.BlockSpec((None, tt, d), lambda b, t: (b, t, 0))
    return pl.pallas_call(
        functools.partial(_cross_kernel, m=m),
        grid=(bsz, seq // tt),
        in_specs=[
            tile,
            _const_spec(g_cross.shape),
            pl.BlockSpec((None,) + qk.shape[1:], lambda b, t: (b, 0, 0)),
            pl.BlockSpec((None,) + vo.shape[1:], lambda b, t: (b, 0, 0)),
        ],
        out_specs=tile,
        out_shape=jax.ShapeDtypeStruct(h.shape, _F32),
        compiler_params=pltpu.CompilerParams(
            dimension_semantics=("arbitrary", "arbitrary"),
            vmem_limit_bytes=VMEM_LIMIT_BYTES),
        name="cross",
    )(h, g_cross, qk, vo)


def _shift_rows(a, first_row):
    rolled = pltpu.roll(a, 1, 0)
    row = lax.broadcasted_iota(jnp.int32, (SUBLANES, a.shape[1]), 0)
    head = jnp.where(row == 0, first_row, rolled[0:SUBLANES])
    return jnp.concatenate([head, rolled[SUBLANES:]], axis=0)


def _ffn_kernel(h_ref, gf_ref, win_ref, cw_ref, cb_ref, wd_ref, gfin_ref, o_ref,
                tail_ref, n_ref, act_ref, *, tt):
    t = pl.program_id(1)

    @pl.when(t == 0)
    def _():
        tail_ref[...] = jnp.zeros(tail_ref.shape, _F32)

    n_ref[...] = _rms(h_ref[...], gf_ref[...]).astype(_BF16)
    for c0 in range(0, D_FF, FF_COLS):
        cols = slice(c0, c0 + FF_COLS)
        u = _dot(n_ref[...], win_ref[:, cols])
        gate = _dot(n_ref[...], win_ref[:, D_FF + c0:D_FF + c0 + FF_COLS])
        w0, w1, w2 = cw_ref[0:1, cols], cw_ref[1:2, cols], cw_ref[2:3, cols]
        prev1 = tail_ref[SUBLANES - 1:SUBLANES, cols]
        prev2 = tail_ref[SUBLANES - 2:SUBLANES - 1, cols]
        acc = w1 * u + _shift_rows(w0 * u, w0 * prev1)
        conv = w2 * u + _shift_rows(acc, w1 * prev1 + w0 * prev2) + cb_ref[:, cols]
        tail_ref[:, cols] = u[tt - SUBLANES:tt, :]
        act_ref[:, cols] = (jax.nn.silu(conv) * gate).astype(_BF16)
    for r0 in range(0, tt, FF_OUT_ROWS):
        rows = slice(r0, r0 + FF_OUT_ROWS)
        h3 = h_ref[rows, :] + _dot(act_ref[rows, :], wd_ref[...])
        o_ref[rows, :] = _rms(h3, gfin_ref[...])


def _ffn(h, g_ffn, w_ffn_in, conv_w, conv_b, w_down, g_final, tt):
    bsz, seq, d = h.shape
    tile = pl.BlockSpec((None, tt, d), lambda b, t: (b, t, 0))
    return pl.pallas_call(
        functools.partial(_ffn_kernel, tt=tt),
        grid=(bsz, seq // tt),
        in_specs=[
            tile,
            _const_spec(g_ffn.shape),
            _const_spec(w_ffn_in.shape),
            _const_spec(conv_w.shape),
            _const_spec(conv_b.shape),
            _const_spec(w_down.shape),
            _const_spec(g_final.shape),
        ],
        out_specs=tile,
        out_shape=jax.ShapeDtypeStruct(h.shape, _F32),
        scratch_shapes=[
            pltpu.VMEM((SUBLANES, D_FF), _F32),
            pltpu.VMEM((tt, d), _BF16),
            pltpu.VMEM((tt, D_FF), _BF16),
        ],
        compiler_params=pltpu.CompilerParams(
            dimension_semantics=("arbitrary", "arbitrary"),
            vmem_limit_bytes=VMEM_LIMIT_BYTES),
        name="ffn",
    )(h, g_ffn, w_ffn_in, conv_w, conv_b, w_down, g_final)


def kernel(x, mem, g_mix, w_in, lower_bounds, attn_sinks, g_onorm, w_branch_a, w_branch_b,
           w_mix_out, g_cross, g_mem, w_cq, w_ckv, w_co, g_ffn, w_ffn_in, conv_w, conv_b,
           w_ffn_down, g_final):
    depth = g_mix.shape[0]
    assert depth == 1 and x.shape[-1] == D_MODEL and x.shape[1] % TOKEN_TILE == 0
    tt = TOKEN_TILE
    h = x
    for l in range(depth):
        qk, vo, (in16, a16, b16, mix16) = _mem_qk_vo(
            mem, g_mem[l][None], w_ckv[l], w_cq[l], w_co[l],
            [w_in[l], w_branch_a[l], w_branch_b[l], w_mix_out[l]])
        h, (ffn_in16, ffn_down16) = _mixer(
            h, g_mix[l][None], in16, lower_bounds.astype(_F32), attn_sinks[l], g_onorm[l][None],
            a16, b16, mix16, [w_ffn_in[l], w_ffn_down[l]], tt)
        h = _cross(h, g_cross[l][None], qk, vo, CROSS_TILE)
        h = _ffn(h, g_ffn[l][None], ffn_in16, conv_w[l], conv_b[l][None], ffn_down16,
                 g_final[None], FFN_TILE)
    return h
```

```python
import functools

import jax
import jax.numpy as jnp
from jax import lax
from jax.experimental import pallas as pl
from jax.experimental.pallas import tpu as pltpu

D_MODEL = 1024
CHUNK = 64
CHUNK_LOG2 = CHUNK.bit_length() - 1
EPS = 1e-6
LOG2_E = 1.4426950408889634

ATT_HEADS = 8
ATT_KV_HEADS = 2
ATT_HEAD_DIM = 64
ATT_GROUP = ATT_HEADS // ATT_KV_HEADS
LOOKBACK = 2
ATT_Q_W = ATT_HEADS * ATT_HEAD_DIM
ATT_KV_W = ATT_KV_HEADS * ATT_HEAD_DIM
KV_HALO = LOOKBACK * CHUNK
ATT_QBLK = 2 * CHUNK
ATT_KBLK = ATT_QBLK + KV_HALO

HG_HEADS = 4
HG_DIM = 128
HG_W = HG_HEADS * HG_DIM
HG_BLK = 4 * CHUNK

X_HEADS = 4
X_HEAD_DIM = D_MODEL // X_HEADS

D_FF = 2816
CONV_WIDTH = 3
FF_COLS = 256
FF_OUT_ROWS = 256
SUBLANES = 8
BF16_ROWS = 16

_OFF_AQ = 0
_OFF_AK = _OFF_AQ + ATT_Q_W
_OFF_AV = _OFF_AK + ATT_KV_W
_OFF_HQ = _OFF_AV + ATT_KV_W
_OFF_HF = _OFF_HQ + HG_W
_OFF_HI = _OFF_HF + HG_W
_OFF_HG = _OFF_HI + HG_W
_OFF_GA = _OFF_HG + HG_W
_OFF_GB = _OFF_GA + D_MODEL
IN_W = _OFF_GB + D_MODEL

TOKEN_TILE = 512
CROSS_TILE = 1024
FFN_TILE = 1024
VMEM_LIMIT_BYTES = 56 * 1024 * 1024

_F32 = jnp.float32
_BF16 = jnp.bfloat16
_NT = (((1,), (1,)), ((), ()))


def _rms(xf, gain):
    return xf * lax.rsqrt(jnp.mean(xf * xf, axis=-1, keepdims=True) + EPS) * gain


def _dot(a, b):
    return jnp.dot(a, b, preferred_element_type=_F32)


def _dot_nt(a, b):
    return lax.dot_general(a, b, _NT, preferred_element_type=_F32)


def _chunk_cumsum(a):
    rows, cols = a.shape
    sub = lax.broadcasted_iota(jnp.int32, (SUBLANES, cols), 0)
    groups = []
    for g0 in range(0, rows, SUBLANES):
        g = a[g0:g0 + SUBLANES, :]
        step = 1
        while step < SUBLANES:
            g = g + jnp.where(sub >= step, pltpu.roll(g, step, 0), 0.0)
            step *= 2
        if g0 % CHUNK:
            g = g + carry
        carry = jnp.broadcast_to(g[SUBLANES - 1:SUBLANES, :], (SUBLANES, cols))
        groups.append(g)
    return jnp.concatenate(groups, axis=0)


def _mixer_kernel(x_ref, gmix_ref, win_ref, lbraw_ref, sinks_ref, gon_ref, wa_ref, wb_ref,
                  wmix_ref, *rest, tt, n_later):
    later_f32 = rest[:n_later]
    o_ref = rest[n_later]
    later_bf16 = rest[n_later + 1:2 * n_later + 1]
    (n_ref, proj_ref, kbuf_ref, kswp_ref, vbuf_ref, vswp_ref,
     st_ref, ao_ref, hy_ref) = rest[2 * n_later + 1:]
    t = pl.program_id(1)
    win = KV_HALO + tt

    @pl.when(t == 0)
    def _():
        for ref in (kbuf_ref, kswp_ref, vbuf_ref, vswp_ref):
            ref[0:KV_HALO, :] = jnp.zeros((KV_HALO, ATT_KV_W), _BF16)
        st_ref[...] = jnp.zeros(st_ref.shape, _F32)

    n_ref[...] = _rms(x_ref[...], gmix_ref[...]).astype(_BF16)
    proj_ref[:, 0:_OFF_HQ] = _dot(n_ref[...], win_ref[:, 0:_OFF_HQ])
    proj_ref[:, _OFF_HQ:_OFF_GA] = _dot(n_ref[...], win_ref[:, _OFF_HQ:_OFF_GA])

    new = slice(KV_HALO, win)
    k32 = proj_ref[:, _OFF_AK:_OFF_AK + ATT_KV_W]
    v32 = proj_ref[:, _OFF_AV:_OFF_AV + ATT_KV_W]
    kbuf_ref[new, :] = k32.astype(_BF16)
    vbuf_ref[new, :] = v32.astype(_BF16)
    kswp_ref[new, :] = pltpu.roll(k32, ATT_HEAD_DIM, 1).astype(_BF16)
    vswp_ref[new, :] = pltpu.roll(v32, ATT_HEAD_DIM, 1).astype(_BF16)

    kvlane = lax.broadcasted_iota(jnp.int32, (win, ATT_KV_W), 1)
    halves = (kvlane < ATT_HEAD_DIM, kvlane >= ATT_HEAD_DIM)
    zero_kv = jnp.zeros((win, ATT_KV_W), _BF16)

    def half_placed(nat_ref, swp_ref, g, par):
        src = nat_ref if g == par else swp_ref
        return jnp.where(halves[par], src[...], zero_kv)

    kz = [[half_placed(kbuf_ref, kswp_ref, g, par) for par in range(2)]
          for g in range(ATT_KV_HEADS)]
    vz = [[half_placed(vbuf_ref, vswp_ref, g, par) for par in range(2)]
          for g in range(ATT_KV_HEADS)]
    qi = lax.broadcasted_iota(jnp.int32, (ATT_QBLK, ATT_KBLK), 0)
    kj = lax.broadcasted_iota(jnp.int32, (ATT_QBLK, ATT_KBLK), 1)
    qc = lax.shift_right_logical(qi, CHUNK_LOG2)
    kc = lax.shift_right_logical(kj, CHUNK_LOG2)
    band = (kc >= qc) & (kc <= qc + LOOKBACK)
    absdist = jnp.abs(qi + KV_HALO - kj).astype(_F32)
    slopes = [2.0 ** (-8.0 * (h + 1) / ATT_HEADS) for h in range(ATT_HEADS)]
    alibi = [jnp.where(band, -(LOG2_E * slope) * absdist, -jnp.inf) for slope in slopes]
    q16 = (proj_ref[:, _OFF_AQ:_OFF_AQ + ATT_Q_W]
           * (LOG2_E * ATT_HEAD_DIM ** -0.5)).astype(_BF16)
    nblk = tt // ATT_QBLK
    probs = [[None] * ATT_HEADS for _ in range(nblk)]
    inv_den = [[None] * ATT_HEADS for _ in range(nblk)]
    for blk in range(nblk):
        r0 = blk * ATT_QBLK
        bias = ([jnp.where((t * tt + kj) >= KV_HALO, a, -jnp.inf) for a in alibi]
                if blk == 0 else alibi)
        for g in range(ATT_KV_HEADS):
            q2 = jnp.concatenate(
                [q16[r0:r0 + ATT_QBLK, 2 * ATT_HEAD_DIM * pair:2 * ATT_HEAD_DIM * (pair + 1)]
                 for pair in (2 * g, 2 * g + 1)], axis=0)
            for par in range(2):
                s2 = _dot_nt(q2, kz[g][par][r0:r0 + ATT_KBLK])
                for half, h in enumerate((ATT_GROUP * g + par, ATT_GROUP * g + par + 2)):
                    s = s2[half * ATT_QBLK:(half + 1) * ATT_QBLK] + bias[h]
                    sink = sinks_ref[h] * LOG2_E
                    m = jnp.maximum(jnp.max(s, axis=-1, keepdims=True), sink)
                    p = jnp.exp2(s - m)
                    inv_den[blk][h] = 1.0 / (
                        jnp.sum(p, axis=-1, keepdims=True) + jnp.exp2(sink - m))
                    probs[blk][h] = p.astype(_BF16)

    proj_ref[:, _OFF_GA:IN_W] = _dot(n_ref[...], win_ref[:, _OFF_GA:IN_W])

    lraw = lbraw_ref[...]
    lexp = jnp.exp(lraw - jnp.max(lraw, axis=0, keepdims=True))
    lb = lexp[0:1, :] / jnp.sum(lexp, axis=0, keepdims=True)
    ri = lax.broadcasted_iota(jnp.int32, (HG_BLK, HG_BLK), 0)
    ci = lax.broadcasted_iota(jnp.int32, (HG_BLK, HG_BLK), 1)
    chunk_causal = ((lax.shift_right_logical(ri, CHUNK_LOG2)
                     == lax.shift_right_logical(ci, CHUNK_LOG2)) & (ri >= ci))
    rowchunk = lax.shift_right_logical(
        lax.broadcasted_iota(jnp.int32, (HG_BLK, HG_DIM), 0), CHUNK_LOG2)
    zero_hd = jnp.zeros((HG_BLK, HG_DIM), _BF16)
    ncb = HG_BLK // CHUNK
    hg_rows = [slice(r0, r0 + HG_BLK) for r0 in range(0, tt, HG_BLK)]

    def chunk_blocks(a):
        return jnp.concatenate([jnp.where(rowchunk == c, a, zero_hd) for c in range(ncb)], axis=1)

    def decayed_operands(rows):
        f = lb + (1.0 - lb) * jax.nn.sigmoid(proj_ref[rows, _OFF_HF:_OFF_HF + HG_W])
        b = _chunk_cumsum(jnp.log2(f))
        b_last = [b[(c + 1) * CHUNK - 1:(c + 1) * CHUNK, :] for c in range(ncb)]
        b_last_rows = jnp.concatenate(
            [jnp.broadcast_to(bl, (CHUNK, HG_W)) for bl in b_last], axis=0)
        q_dec = (proj_ref[rows, _OFF_HQ:_OFF_HQ + HG_W] * (HG_DIM ** -0.5)
                 * jnp.exp2(b)).astype(_BF16)
        k_inv = ((1.0 - f) * jnp.exp2(-b)).astype(_BF16)
        k_end = ((1.0 - f) * jnp.exp2(b_last_rows - b)).astype(_BF16)
        return q_dec, k_inv, k_end, [jnp.exp2(bl) for bl in b_last]

    operands = [decayed_operands(rows) for rows in hg_rows]

    olane = lax.broadcasted_iota(jnp.int32, (ATT_QBLK, 2 * ATT_HEAD_DIM), 1)
    for blk in range(nblk):
        r0 = blk * ATT_QBLK
        for g in range(ATT_KV_HEADS):
            h0 = ATT_GROUP * g
            o4 = (_dot(jnp.concatenate([probs[blk][h0], probs[blk][h0 + 2]], axis=0),
                       vz[g][0][r0:r0 + ATT_KBLK])
                  + _dot(jnp.concatenate([probs[blk][h0 + 1], probs[blk][h0 + 3]], axis=0),
                         vz[g][1][r0:r0 + ATT_KBLK]))
            for half in range(2):
                he, pair = h0 + 2 * half, 2 * g + half
                scale2 = jnp.where(olane < ATT_HEAD_DIM, inv_den[blk][he], inv_den[blk][he + 1])
                o2 = o4[half * ATT_QBLK:(half + 1) * ATT_QBLK]
                ao_ref[r0:r0 + ATT_QBLK, 2 * ATT_HEAD_DIM * pair:2 * ATT_HEAD_DIM * (pair + 1)] = (
                    (o2 * scale2).astype(_BF16))

    def state_free_dots(rows, q_dec, k_inv, k_end):
        a_mats, ds_all, v16 = [], [], []
        for hh in range(HG_HEADS):
            sl = slice(hh * HG_DIM, (hh + 1) * HG_DIM)
            a_mats.append(
                jnp.where(chunk_causal, _dot_nt(q_dec[:, sl], k_inv[:, sl]), 0.0).astype(_BF16))
            v = proj_ref[rows, _OFF_HI + hh * HG_DIM:_OFF_HI + (hh + 1) * HG_DIM]
            v16.append(v.astype(_BF16))
            ds_all.append(_dot(v.T.astype(_BF16), chunk_blocks(k_end[:, sl])))
        return a_mats, ds_all, v16

    local = [state_free_dots(rows, *ops[:3]) for rows, ops in zip(hg_rows, operands)]

    ya = _dot(ao_ref[...], wa_ref[...])

    gon = gon_ref[...]
    state = [st_ref[hh] for hh in range(HG_HEADS)]
    for rows, (q_dec, _, _, decay), (a_mats, ds_all, v16) in zip(hg_rows, operands, local):
        for hh in range(HG_HEADS):
            sl = slice(hh * HG_DIM, (hh + 1) * HG_DIM)
            st = state[hh]
            entering = []
            for c in range(ncb):
                entering.append(st.astype(_BF16))
                st = st * decay[c][:, sl] + ds_all[hh][:, c * HG_DIM:(c + 1) * HG_DIM]
            state[hh] = st
            o = (_dot(a_mats[hh], v16[hh])
                 + _dot_nt(chunk_blocks(q_dec[:, sl]), jnp.concatenate(entering, axis=1)))
            gate = jax.nn.silu(proj_ref[rows, _OFF_HG + hh * HG_DIM:_OFF_HG + (hh + 1) * HG_DIM])
            y = o * lax.rsqrt(jnp.mean(o * o, axis=-1, keepdims=True) + EPS) * gon * gate
            hy_ref[rows, sl] = y.astype(_BF16)
    for hh in range(HG_HEADS):
        st_ref[hh] = state[hh]

    yb = _dot(hy_ref[...], wb_ref[...])
    y = (jax.nn.sigmoid(proj_ref[:, _OFF_GA:_OFF_GA + D_MODEL]) * ya
         + jax.nn.sigmoid(proj_ref[:, _OFF_GB:_OFF_GB + D_MODEL]) * yb)
    o_ref[...] = x_ref[...] + _dot(y.astype(_BF16), wmix_ref[...])

    for ref in (kbuf_ref, kswp_ref, vbuf_ref, vswp_ref):
        ref[0:KV_HALO, :] = ref[tt:tt + KV_HALO, :]
    _cast_slabs(later_f32, later_bf16)


def _const_spec(shape):
    return pl.BlockSpec(shape, lambda b, t: (0,) * len(shape), pipeline_mode=pl.Buffered(1))


def _slab_spec(shape, nsteps, linear_step):
    rows, cols = shape
    slab = next(s for s in range(BF16_ROWS, rows + 1, BF16_ROWS)
                if rows % s == 0 and rows // s <= nsteps)
    nslabs = rows // slab
    return pl.BlockSpec((slab, cols), lambda *ids: (linear_step(*ids) * nslabs // nsteps, 0))


def _cast_slabs(later_f32, later_bf16):
    for src, dst in zip(later_f32, later_bf16):
        dst[...] = src[...].astype(_BF16)


def _mixer(x, g_mix, w_in, lb_raw, sinks, g_onorm, w_a, w_b, w_mix, later_weights, tt):
    bsz, seq, d = x.shape
    nt = seq // tt
    tile = pl.BlockSpec((None, tt, d), lambda b, t: (b, t, 0))
    slabs = [_slab_spec(w.shape, bsz * nt, lambda b, t: b * nt + t) for w in later_weights]
    h, *later16 = pl.pallas_call(
        functools.partial(_mixer_kernel, tt=tt, n_later=len(later_weights)),
        grid=(bsz, nt),
        in_specs=[
            tile,
            _const_spec(g_mix.shape),
            _const_spec(w_in.shape),
            _const_spec(lb_raw.shape),
            pl.BlockSpec(memory_space=pltpu.SMEM),
            _const_spec(g_onorm.shape),
            _const_spec(w_a.shape),
            _const_spec(w_b.shape),
            _const_spec(w_mix.shape),
        ] + slabs,
        out_specs=[tile] + slabs,
        out_shape=[jax.ShapeDtypeStruct(x.shape, _F32)]
        + [jax.ShapeDtypeStruct(w.shape, _BF16) for w in later_weights],
        scratch_shapes=[
            pltpu.VMEM((tt, d), _BF16),
            pltpu.VMEM((tt, IN_W), _F32),
            pltpu.VMEM((KV_HALO + tt, ATT_KV_W), _BF16),
            pltpu.VMEM((KV_HALO + tt, ATT_KV_W), _BF16),
            pltpu.VMEM((KV_HALO + tt, ATT_KV_W), _BF16),
            pltpu.VMEM((KV_HALO + tt, ATT_KV_W), _BF16),
            pltpu.VMEM((HG_HEADS, HG_DIM, HG_DIM), _F32),
            pltpu.VMEM((tt, ATT_Q_W), _BF16),
            pltpu.VMEM((tt, HG_W), _BF16),
        ],
        compiler_params=pltpu.CompilerParams(
            dimension_semantics=("arbitrary", "arbitrary"),
            vmem_limit_bytes=VMEM_LIMIT_BYTES),
        name="mixer",
    )(x, g_mix, w_in, lb_raw, sinks, g_onorm, w_a, w_b, w_mix, *later_weights)
    return h, later16


def _mem_qk_vo_kernel(mem_ref, gmem_ref, wkv_ref, wq_ref, wo_ref, *rest, n_later, m):
    later_f32 = rest[:n_later]
    qk_ref, vo_ref = rest[n_later:n_later + 2]
    later_bf16 = rest[n_later + 2:2 * n_later + 2]
    wkv16_ref, wq16_ref, wo16_ref = rest[2 * n_later + 2:]

    @pl.when(pl.program_id(0) == 0)
    def _():
        wkv16_ref[...] = wkv_ref[...].astype(_BF16)
        wq16_ref[...] = wq_ref[...].astype(_BF16)
        wo16_ref[...] = wo_ref[...].astype(_BF16)

    nm = _rms(mem_ref[...], gmem_ref[...]).astype(_BF16)
    kv = _dot(nm, wkv16_ref[...]).astype(_BF16)
    for hd in range(X_HEADS):
        dims = slice(hd * X_HEAD_DIM, (hd + 1) * X_HEAD_DIM)
        vdims = slice(D_MODEL + hd * X_HEAD_DIM, D_MODEL + (hd + 1) * X_HEAD_DIM)
        mems = slice(hd * m, (hd + 1) * m)
        qk_ref[:, mems] = _dot_nt(wq16_ref[:, dims], kv[:, dims]).astype(_BF16)
        vo_ref[mems, :] = _dot(kv[:, vdims], wo16_ref[dims, :]).astype(_BF16)
    _cast_slabs(later_f32, later_bf16)


def _mem_qk_vo(mem, g_mem, w_ckv, w_cq, w_co, later_weights):
    bsz, m, d = mem.shape
    slabs = [_slab_spec(w.shape, bsz, lambda b: b) for w in later_weights]
    resident = lambda w: pl.BlockSpec(w.shape, lambda b: (0, 0), pipeline_mode=pl.Buffered(1))
    qk, vo, *later16 = pl.pallas_call(
        functools.partial(_mem_qk_vo_kernel, n_later=len(later_weights), m=m),
        grid=(bsz,),
        in_specs=[
            pl.BlockSpec((None, m, d), lambda b: (b, 0, 0)),
            pl.BlockSpec(g_mem.shape, lambda b: (0, 0)),
            resident(w_ckv), resident(w_cq), resident(w_co),
        ] + slabs,
        out_specs=[pl.BlockSpec((None, d, X_HEADS * m), lambda b: (b, 0, 0)),
                   pl.BlockSpec((None, X_HEADS * m, d), lambda b: (b, 0, 0))] + slabs,
        out_shape=[jax.ShapeDtypeStruct((bsz, d, X_HEADS * m), _BF16),
                   jax.ShapeDtypeStruct((bsz, X_HEADS * m, d), _BF16)]
        + [jax.ShapeDtypeStruct(w.shape, _BF16) for w in later_weights],
        scratch_shapes=[pltpu.VMEM(w.shape, _BF16) for w in (w_ckv, w_cq, w_co)],
        compiler_params=pltpu.CompilerParams(
            dimension_semantics=("arbitrary",), vmem_limit_bytes=VMEM_LIMIT_BYTES),
        name="mem_qk_vo",
    )(mem, g_mem, w_ckv, w_cq, w_co, *later_weights)
    return qk, vo, later16


def _cross_kernel(h_ref, gc_ref, qk_ref, vo_ref, o_ref, *, m):
    h = h_ref[...]
    s_all = _dot(_rms(h, gc_ref[...]).astype(_BF16), qk_ref[...]) * (LOG2_E * X_HEAD_DIM ** -0.5)
    probs = []
    for hd in range(X_HEADS):
        s = s_all[:, hd * m:(hd + 1) * m]
        p = jnp.exp2(s - jnp.max(s, axis=-1, keepdims=True))
        probs.append((p * (1.0 / jnp.sum(p, axis=-1, keepdims=True))).astype(_BF16))
    o_ref[...] = h + _dot(jnp.concatenate(probs, axis=-1), vo_ref[...])


def _cross(h, g_cross, qk, vo, tt):
    bsz, seq, d = h.shape
    m = vo.shape[1] // X_HEADS
    tile = pl.BlockSpec((None, tt, d), lambda b, t: (b, t, 0))
    return pl.pallas_call(
        functools.partial(_cross_kernel, m=m),
        grid=(bsz, seq // tt),
        in_specs=[
            tile,
            _const_spec(g_cross.shape),
            pl.BlockSpec((None,) + qk.shape[1:], lambda b, t: (b, 0, 0)),
            pl.BlockSpec((None,) + vo.shape[1:], lambda b, t: (b, 0, 0)),
        ],
        out_specs=tile,
        out_shape=jax.ShapeDtypeStruct(h.shape, _F32),
        compiler_params=pltpu.CompilerParams(
            dimension_semantics=("arbitrary", "arbitrary"),
            vmem_limit_bytes=VMEM_LIMIT_BYTES),
        name="cross",
    )(h, g_cross, qk, vo)


def _shift_rows(a, first_row):
    rolled = pltpu.roll(a, 1, 0)
    row = lax.broadcasted_iota(jnp.int32, (SUBLANES, a.shape[1]), 0)
    head = jnp.where(row == 0, first_row, rolled[0:SUBLANES])
    return jnp.concatenate([head, rolled[SUBLANES:]], axis=0)


def _ffn_kernel(h_ref, gf_ref, win_ref, cw_ref, cb_ref, wd_ref, gfin_ref, o_ref,
                tail_ref, n_ref, act_ref, *, tt):
    t = pl.program_id(1)

    @pl.when(t == 0)
    def _():
        tail_ref[...] = jnp.zeros(tail_ref.shape, _F32)

    n_ref[...] = _rms(h_ref[...], gf_ref[...]).astype(_BF16)
    for c0 in range(0, D_FF, FF_COLS):
        cols = slice(c0, c0 + FF_COLS)
        u = _dot(n_ref[...], win_ref[:, cols])
        gate = _dot(n_ref[...], win_ref[:, D_FF + c0:D_FF + c0 + FF_COLS])
        w0, w1, w2 = cw_ref[0:1, cols], cw_ref[1:2, cols], cw_ref[2:3, cols]
        prev1 = tail_ref[SUBLANES - 1:SUBLANES, cols]
        prev2 = tail_ref[SUBLANES - 2:SUBLANES - 1, cols]
        acc = w1 * u + _shift_rows(w0 * u, w0 * prev1)
        conv = w2 * u + _shift_rows(acc, w1 * prev1 + w0 * prev2) + cb_ref[:, cols]
        tail_ref[:, cols] = u[tt - SUBLANES:tt, :]
        act_ref[:, cols] = (jax.nn.silu(conv) * gate).astype(_BF16)
    for r0 in range(0, tt, FF_OUT_ROWS):
        rows = slice(r0, r0 + FF_OUT_ROWS)
        h3 = h_ref[rows, :] + _dot(act_ref[rows, :], wd_ref[...])
        o_ref[rows, :] = _rms(h3, gfin_ref[...])


def _ffn(h, g_ffn, w_ffn_in, conv_w, conv_b, w_down, g_final, tt):
    bsz, seq, d = h.shape
    tile = pl.BlockSpec((None, tt, d), lambda b, t: (b, t, 0))
    return pl.pallas_call(
        functools.partial(_ffn_kernel, tt=tt),
        grid=(bsz, seq // tt),
        in_specs=[
            tile,
            _const_spec(g_ffn.shape),
            _const_spec(w_ffn_in.shape),
            _const_spec(conv_w.shape),
            _const_spec(conv_b.shape),
            _const_spec(w_down.shape),
            _const_spec(g_final.shape),
        ],
        out_specs=tile,
        out_shape=jax.ShapeDtypeStruct(h.shape, _F32),
        scratch_shapes=[
            pltpu.VMEM((SUBLANES, D_FF), _F32),
            pltpu.VMEM((tt, d), _BF16),
            pltpu.VMEM((tt, D_FF), _BF16),
        ],
        compiler_params=pltpu.CompilerParams(
            dimension_semantics=("arbitrary", "arbitrary"),
            vmem_limit_bytes=VMEM_LIMIT_BYTES),
        name="ffn",
    )(h, g_ffn, w_ffn_in, conv_w, conv_b, w_down, g_final)


def kernel(x, mem, g_mix, w_in, lower_bounds, attn_sinks, g_onorm, w_branch_a, w_branch_b,
           w_mix_out, g_cross, g_mem, w_cq, w_ckv, w_co, g_ffn, w_ffn_in, conv_w, conv_b,
           w_ffn_down, g_final):
    depth = g_mix.shape[0]
    assert depth == 1 and x.shape[-1] == D_MODEL and x.shape[1] % TOKEN_TILE == 0
    tt = TOKEN_TILE
    h = x
    for l in range(depth):
        qk, vo, (in16, a16, b16, mix16) = _mem_qk_vo(
            mem, g_mem[l][None], w_ckv[l], w_cq[l], w_co[l],
            [w_in[l], w_branch_a[l], w_branch_b[l], w_mix_out[l]])
        h, (ffn_in16, ffn_down16) = _mixer(
            h, g_mix[l][None], in16, lower_bounds.astype(_F32), attn_sinks[l], g_onorm[l][None],
            a16, b16, mix16, [w_ffn_in[l], w_ffn_down[l]], tt)
        h = _cross(h, g_cross[l][None], qk, vo, CROSS_TILE)
        h = _ffn(h, g_ffn[l][None], ffn_in16, conv_w[l], conv_b[l][None], ffn_down16,
                 g_final[None], FFN_TILE)
    return h
```

```python
import functools

import jax
import jax.numpy as jnp
from jax import lax
from jax.experimental import pallas as pl
from jax.experimental.pallas import tpu as pltpu

D_MODEL = 1024
CHUNK = 64
CHUNK_LOG2 = CHUNK.bit_length() - 1
EPS = 1e-6
LOG2_E = 1.4426950408889634

ATT_HEADS = 8
ATT_KV_HEADS = 2
ATT_HEAD_DIM = 64
ATT_GROUP = ATT_HEADS // ATT_KV_HEADS
LOOKBACK = 2
ATT_Q_W = ATT_HEADS * ATT_HEAD_DIM
ATT_KV_W = ATT_KV_HEADS * ATT_HEAD_DIM
KV_HALO = LOOKBACK * CHUNK
ATT_QBLK = 2 * CHUNK
ATT_KBLK = ATT_QBLK + KV_HALO

HG_HEADS = 4
HG_DIM = 128
HG_W = HG_HEADS * HG_DIM
HG_BLK = 4 * CHUNK

X_HEADS = 4
X_HEAD_DIM = D_MODEL // X_HEADS

D_FF = 2816
CONV_WIDTH = 3
FF_COLS = 256
FF_OUT_ROWS = 256
SUBLANES = 8
BF16_ROWS = 16

_OFF_AQ = 0
_OFF_AK = _OFF_AQ + ATT_Q_W
_OFF_AV = _OFF_AK + ATT_KV_W
_OFF_HQ = _OFF_AV + ATT_KV_W
_OFF_HF = _OFF_HQ + HG_W
_OFF_HI = _OFF_HF + HG_W
_OFF_HG = _OFF_HI + HG_W
_OFF_GA = _OFF_HG + HG_W
_OFF_GB = _OFF_GA + D_MODEL
IN_W = _OFF_GB + D_MODEL

TOKEN_TILE = 512
CROSS_TILE = 1024
FFN_TILE = 1024
VMEM_LIMIT_BYTES = 56 * 1024 * 1024

_F32 = jnp.float32
_BF16 = jnp.bfloat16
_NT = (((1,), (1,)), ((), ()))


def _rms(xf, gain):
    return xf * lax.rsqrt(jnp.mean(xf * xf, axis=-1, keepdims=True) + EPS) * gain


def _dot(a, b):
    return jnp.dot(a, b, preferred_element_type=_F32)


def _dot_nt(a, b):
    return lax.dot_general(a, b, _NT, preferred_element_type=_F32)


def _chunk_cumsum(a):
    rows, cols = a.shape
    sub = lax.broadcasted_iota(jnp.int32, (SUBLANES, cols), 0)
    groups = []
    for g0 in range(0, rows, SUBLANES):
        g = a[g0:g0 + SUBLANES, :]
        step = 1
        while step < SUBLANES:
            g = g + jnp.where(sub >= step, pltpu.roll(g, step, 0), 0.0)
            step *= 2
        if g0 % CHUNK:
            g = g + carry
        carry = jnp.broadcast_to(g[SUBLANES - 1:SUBLANES, :], (SUBLANES, cols))
        groups.append(g)
    return jnp.concatenate(groups, axis=0)


def _mixer_kernel(x_ref, gmix_ref, win_ref, lbraw_ref, sinks_ref, gon_ref, wa_ref, wb_ref,
                  wmix_ref, *rest, tt, n_later):
    later_f32 = rest[:n_later]
    o_ref = rest[n_later]
    later_bf16 = rest[n_later + 1:2 * n_later + 1]
    n_ref, proj_ref, kz_ref, vz_ref, st_ref, ao_ref, hy_ref = rest[2 * n_later + 1:]
    t = pl.program_id(1)
    win = KV_HALO + tt

    @pl.when(t == 0)
    def _():
        for ref in (kz_ref, vz_ref):
            ref[:, :, 0:KV_HALO, :] = jnp.zeros((ATT_KV_HEADS, 2, KV_HALO, ATT_KV_W), _BF16)
        st_ref[...] = jnp.zeros(st_ref.shape, _F32)

    n_ref[...] = _rms(x_ref[...], gmix_ref[...]).astype(_BF16)
    proj_ref[:, 0:_OFF_HQ] = _dot(n_ref[...], win_ref[:, 0:_OFF_HQ])
    proj_ref[:, _OFF_HQ:_OFF_GA] = _dot(n_ref[...], win_ref[:, _OFF_HQ:_OFF_GA])

    new = slice(KV_HALO, win)
    kvlane = lax.broadcasted_iota(jnp.int32, (tt, ATT_KV_W), 1)
    halves = (kvlane < ATT_HEAD_DIM, kvlane >= ATT_HEAD_DIM)
    for dst_ref, off in ((kz_ref, _OFF_AK), (vz_ref, _OFF_AV)):
        nat = proj_ref[:, off:off + ATT_KV_W]
        swapped = pltpu.roll(nat, ATT_HEAD_DIM, 1)
        for g in range(ATT_KV_HEADS):
            for par in range(2):
                dst_ref[g, par, new, :] = jnp.where(
                    halves[par], nat if g == par else swapped, 0.0).astype(_BF16)
    qi = lax.broadcasted_iota(jnp.int32, (ATT_QBLK, ATT_KBLK), 0)
    kj = lax.broadcasted_iota(jnp.int32, (ATT_QBLK, ATT_KBLK), 1)
    qc = lax.shift_right_logical(qi, CHUNK_LOG2)
    kc = lax.shift_right_logical(kj, CHUNK_LOG2)
    band = (kc >= qc) & (kc <= qc + LOOKBACK)
    absdist = jnp.abs(qi + KV_HALO - kj).astype(_F32)
    slopes = [2.0 ** (-8.0 * (h + 1) / ATT_HEADS) for h in range(ATT_HEADS)]
    alibi = [jnp.where(band, -(LOG2_E * slope) * absdist, -jnp.inf) for slope in slopes]
    q16 = (proj_ref[:, _OFF_AQ:_OFF_AQ + ATT_Q_W]
           * (LOG2_E * ATT_HEAD_DIM ** -0.5)).astype(_BF16)
    nblk = tt // ATT_QBLK
    probs = [[None] * ATT_HEADS for _ in range(nblk)]
    inv_den = [[None] * ATT_HEADS for _ in range(nblk)]
    for blk in range(nblk):
        r0 = blk * ATT_QBLK
        bias = ([jnp.where((t * tt + kj) >= KV_HALO, a, -jnp.inf) for a in alibi]
                if blk == 0 else alibi)
        for g in range(ATT_KV_HEADS):
            q2 = jnp.concatenate(
                [q16[r0:r0 + ATT_QBLK, 2 * ATT_HEAD_DIM * pair:2 * ATT_HEAD_DIM * (pair + 1)]
                 for pair in (2 * g, 2 * g + 1)], axis=0)
            for par in range(2):
                s2 = _dot_nt(q2, kz_ref[g, par, r0:r0 + ATT_KBLK, :])
                for half, h in enumerate((ATT_GROUP * g + par, ATT_GROUP * g + par + 2)):
                    s = s2[half * ATT_QBLK:(half + 1) * ATT_QBLK] + bias[h]
                    sink = sinks_ref[h] * LOG2_E
                    m = jnp.maximum(jnp.max(s, axis=-1, keepdims=True), sink)
                    p = jnp.exp2(s - m)
                    inv_den[blk][h] = 1.0 / (
                        jnp.sum(p, axis=-1, keepdims=True) + jnp.exp2(sink - m))
                    probs[blk][h] = p.astype(_BF16)

    proj_ref[:, _OFF_GA:IN_W] = _dot(n_ref[...], win_ref[:, _OFF_GA:IN_W])

    lraw = lbraw_ref[...]
    lexp = jnp.exp(lraw - jnp.max(lraw, axis=0, keepdims=True))
    lb = lexp[0:1, :] / jnp.sum(lexp, axis=0, keepdims=True)
    ri = lax.broadcasted_iota(jnp.int32, (HG_BLK, HG_BLK), 0)
    ci = lax.broadcasted_iota(jnp.int32, (HG_BLK, HG_BLK), 1)
    chunk_causal = ((lax.shift_right_logical(ri, CHUNK_LOG2)
                     == lax.shift_right_logical(ci, CHUNK_LOG2)) & (ri >= ci))
    rowchunk = lax.shift_right_logical(
        lax.broadcasted_iota(jnp.int32, (HG_BLK, HG_DIM), 0), CHUNK_LOG2)
    zero_hd = jnp.zeros((HG_BLK, HG_DIM), _BF16)
    ncb = HG_BLK // CHUNK
    hg_rows = [slice(r0, r0 + HG_BLK) for r0 in range(0, tt, HG_BLK)]

    def chunk_blocks(a):
        return jnp.concatenate([jnp.where(rowchunk == c, a, zero_hd) for c in range(ncb)], axis=1)

    def decayed_operands(rows):
        f = lb + (1.0 - lb) * jax.nn.sigmoid(proj_ref[rows, _OFF_HF:_OFF_HF + HG_W])
        b = _chunk_cumsum(jnp.log2(f))
        b_last = [b[(c + 1) * CHUNK - 1:(c + 1) * CHUNK, :] for c in range(ncb)]
        b_last_rows = jnp.concatenate(
            [jnp.broadcast_to(bl, (CHUNK, HG_W)) for bl in b_last], axis=0)
        q_dec = (proj_ref[rows, _OFF_HQ:_OFF_HQ + HG_W] * (HG_DIM ** -0.5)
                 * jnp.exp2(b)).astype(_BF16)
        k_inv = ((1.0 - f) * jnp.exp2(-b)).astype(_BF16)
        k_end = ((1.0 - f) * jnp.exp2(b_last_rows - b)).astype(_BF16)
        return q_dec, k_inv, k_end, [jnp.exp2(bl) for bl in b_last]

    operands = [decayed_operands(rows) for rows in hg_rows]

    olane = lax.broadcasted_iota(jnp.int32, (ATT_QBLK, 2 * ATT_HEAD_DIM), 1)
    for blk in range(nblk):
        r0 = blk * ATT_QBLK
        for g in range(ATT_KV_HEADS):
            h0 = ATT_GROUP * g
            o4 = (_dot(jnp.concatenate([probs[blk][h0], probs[blk][h0 + 2]], axis=0),
                       vz_ref[g, 0, r0:r0 + ATT_KBLK, :])
                  + _dot(jnp.concatenate([probs[blk][h0 + 1], probs[blk][h0 + 3]], axis=0),
                         vz_ref[g, 1, r0:r0 + ATT_KBLK, :]))
            for half in range(2):
                he, pair = h0 + 2 * half, 2 * g + half
                scale2 = jnp.where(olane < ATT_HEAD_DIM, inv_den[blk][he], inv_den[blk][he + 1])
                o2 = o4[half * ATT_QBLK:(half + 1) * ATT_QBLK]
                ao_ref[r0:r0 + ATT_QBLK, 2 * ATT_HEAD_DIM * pair:2 * ATT_HEAD_DIM * (pair + 1)] = (
                    (o2 * scale2).astype(_BF16))

    def state_free_dots(rows, q_dec, k_inv, k_end):
        a_mats, ds_all, v16 = [], [], []
        for hh in range(HG_HEADS):
            sl = slice(hh * HG_DIM, (hh + 1) * HG_DIM)
            a_mats.append(
                jnp.where(chunk_causal, _dot_nt(q_dec[:, sl], k_inv[:, sl]), 0.0).astype(_BF16))
            v = proj_ref[rows, _OFF_HI + hh * HG_DIM:_OFF_HI + (hh + 1) * HG_DIM]
            v16.append(v.astype(_BF16))
            ds_all.append(_dot(v.T.astype(_BF16), chunk_blocks(k_end[:, sl])))
        return a_mats, ds_all, v16

    local = [state_free_dots(rows, *ops[:3]) for rows, ops in zip(hg_rows, operands)]

    ya = _dot(ao_ref[...], wa_ref[...])

    gon = gon_ref[...]
    state = [st_ref[hh] for hh in range(HG_HEADS)]
    for rows, (q_dec, _, _, decay), (a_mats, ds_all, v16) in zip(hg_rows, operands, local):
        for hh in range(HG_HEADS):
            sl = slice(hh * HG_DIM, (hh + 1) * HG_DIM)
            st = state[hh]
            entering = []
            for c in range(ncb):
                entering.append(st.astype(_BF16))
                st = st * decay[c][:, sl] + ds_all[hh][:, c * HG_DIM:(c + 1) * HG_DIM]
            state[hh] = st
            o = (_dot(a_mats[hh], v16[hh])
                 + _dot_nt(chunk_blocks(q_dec[:, sl]), jnp.concatenate(entering, axis=1)))
            gate = jax.nn.silu(proj_ref[rows, _OFF_HG + hh * HG_DIM:_OFF_HG + (hh + 1) * HG_DIM])
            y = o * lax.rsqrt(jnp.mean(o * o, axis=-1, keepdims=True) + EPS) * gon * gate
            hy_ref[rows, sl] = y.astype(_BF16)
    for hh in range(HG_HEADS):
        st_ref[hh] = state[hh]

    yb = _dot(hy_ref[...], wb_ref[...])
    y = (jax.nn.sigmoid(proj_ref[:, _OFF_GA:_OFF_GA + D_MODEL]) * ya
         + jax.nn.sigmoid(proj_ref[:, _OFF_GB:_OFF_GB + D_MODEL]) * yb)
    o_ref[...] = x_ref[...] + _dot(y.astype(_BF16), wmix_ref[...])

    for ref in (kz_ref, vz_ref):
        ref[:, :, 0:KV_HALO, :] = ref[:, :, tt:tt + KV_HALO, :]
    _cast_slabs(later_f32, later_bf16)


def _const_spec(shape):
    return pl.BlockSpec(shape, lambda b, t: (0,) * len(shape), pipeline_mode=pl.Buffered(1))


def _slab_spec(shape, nsteps, linear_step):
    rows, cols = shape
    slab = next(s for s in range(BF16_ROWS, rows + 1, BF16_ROWS)
                if rows % s == 0 and rows // s <= nsteps)
    nslabs = rows // slab
    return pl.BlockSpec((slab, cols), lambda *ids: (linear_step(*ids) * nslabs // nsteps, 0))


def _cast_slabs(later_f32, later_bf16):
    for src, dst in zip(later_f32, later_bf16):
        dst[...] = src[...].astype(_BF16)


def _mixer(x, g_mix, w_in, lb_raw, sinks, g_onorm, w_a, w_b, w_mix, later_weights, tt):
    bsz, seq, d = x.shape
    nt = seq // tt
    tile = pl.BlockSpec((None, tt, d), lambda b, t: (b, t, 0))
    slabs = [_slab_spec(w.shape, bsz * nt, lambda b, t: b * nt + t) for w in later_weights]
    h, *later16 = pl.pallas_call(
        functools.partial(_mixer_kernel, tt=tt, n_later=len(later_weights)),
        grid=(bsz, nt),
        in_specs=[
            tile,
            _const_spec(g_mix.shape),
            _const_spec(w_in.shape),
            _const_spec(lb_raw.shape),
            pl.BlockSpec(memory_space=pltpu.SMEM),
            _const_spec(g_onorm.shape),
            _const_spec(w_a.shape),
            _const_spec(w_b.shape),
            _const_spec(w_mix.shape),
        ] + slabs,
        out_specs=[tile] + slabs,
        out_shape=[jax.ShapeDtypeStruct(x.shape, _F32)]
        + [jax.ShapeDtypeStruct(w.shape, _BF16) for w in later_weights],
        scratch_shapes=[
            pltpu.VMEM((tt, d), _BF16),
            pltpu.VMEM((tt, IN_W), _F32),
            pltpu.VMEM((ATT_KV_HEADS, 2, KV_HALO + tt, ATT_KV_W), _BF16),
            pltpu.VMEM((ATT_KV_HEADS, 2, KV_HALO + tt, ATT_KV_W), _BF16),
            pltpu.VMEM((HG_HEADS, HG_DIM, HG_DIM), _F32),
            pltpu.VMEM((tt, ATT_Q_W), _BF16),
            pltpu.VMEM((tt, HG_W), _BF16),
        ],
        compiler_params=pltpu.CompilerParams(
            dimension_semantics=("arbitrary", "arbitrary"),
            vmem_limit_bytes=VMEM_LIMIT_BYTES),
        name="mixer",
    )(x, g_mix, w_in, lb_raw, sinks, g_onorm, w_a, w_b, w_mix, *later_weights)
    return h, later16


def _mem_qk_vo_kernel(mem_ref, gmem_ref, wkv_ref, wq_ref, wo_ref, *rest, n_later, m):
    later_f32 = rest[:n_later]
    qk_ref, vo_ref = rest[n_later:n_later + 2]
    later_bf16 = rest[n_later + 2:2 * n_later + 2]
    wkv16_ref, wq16_ref, wo16_ref = rest[2 * n_later + 2:]

    @pl.when(pl.program_id(0) == 0)
    def _():
        wkv16_ref[...] = wkv_ref[...].astype(_BF16)
        wq16_ref[...] = wq_ref[...].astype(_BF16)
        wo16_ref[...] = wo_ref[...].astype(_BF16)

    nm = _rms(mem_ref[...], gmem_ref[...]).astype(_BF16)
    kv = _dot(nm, wkv16_ref[...]).astype(_BF16)
    for hd in range(X_HEADS):
        dims = slice(hd * X_HEAD_DIM, (hd + 1) * X_HEAD_DIM)
        vdims = slice(D_MODEL + hd * X_HEAD_DIM, D_MODEL + (hd + 1) * X_HEAD_DIM)
        mems = slice(hd * m, (hd + 1) * m)
        qk_ref[:, mems] = _dot_nt(wq16_ref[:, dims], kv[:, dims]).astype(_BF16)
        vo_ref[mems, :] = _dot(kv[:, vdims], wo16_ref[dims, :]).astype(_BF16)
    _cast_slabs(later_f32, later_bf16)


def _mem_qk_vo(mem, g_mem, w_ckv, w_cq, w_co, later_weights):
    bsz, m, d = mem.shape
    slabs = [_slab_spec(w.shape, bsz, lambda b: b) for w in later_weights]
    resident = lambda w: pl.BlockSpec(w.shape, lambda b: (0, 0), pipeline_mode=pl.Buffered(1))
    qk, vo, *later16 = pl.pallas_call(
        functools.partial(_mem_qk_vo_kernel, n_later=len(later_weights), m=m),
        grid=(bsz,),
        in_specs=[
            pl.BlockSpec((None, m, d), lambda b: (b, 0, 0)),
            pl.BlockSpec(g_mem.shape, lambda b: (0, 0)),
            resident(w_ckv), resident(w_cq), resident(w_co),
        ] + slabs,
        out_specs=[pl.BlockSpec((None, d, X_HEADS * m), lambda b: (b, 0, 0)),
                   pl.BlockSpec((None, X_HEADS * m, d), lambda b: (b, 0, 0))] + slabs,
        out_shape=[jax.ShapeDtypeStruct((bsz, d, X_HEADS * m), _BF16),
                   jax.ShapeDtypeStruct((bsz, X_HEADS * m, d), _BF16)]
        + [jax.ShapeDtypeStruct(w.shape, _BF16) for w in later_weights],
        scratch_shapes=[pltpu.VMEM(w.shape, _BF16) for w in (w_ckv, w_cq, w_co)],
        compiler_params=pltpu.CompilerParams(
            dimension_semantics=("arbitrary",), vmem_limit_bytes=VMEM_LIMIT_BYTES),
        name="mem_qk_vo",
    )(mem, g_mem, w_ckv, w_cq, w_co, *later_weights)
    return qk, vo, later16


def _cross_kernel(h_ref, gc_ref, qk_ref, vo_ref, o_ref, *, m):
    h = h_ref[...]
    s_all = _dot(_rms(h, gc_ref[...]).astype(_BF16), qk_ref[...]) * (LOG2_E * X_HEAD_DIM ** -0.5)
    probs = []
    for hd in range(X_HEADS):
        s = s_all[:, hd * m:(hd + 1) * m]
        p = jnp.exp2(s - jnp.max(s, axis=-1, keepdims=True))
        probs.append((p * (1.0 / jnp.sum(p, axis=-1, keepdims=True))).astype(_BF16))
    o_ref[...] = h + _dot(jnp.concatenate(probs, axis=-1), vo_ref[...])


def _cross(h, g_cross, qk, vo, tt):
    bsz, seq, d = h.shape
    m = vo.shape[1] // X_HEADS
    tile = pl.BlockSpec((None, tt, d), lambda b, t: (b, t, 0))
    return pl.pallas_call(
        functools.partial(_cross_kernel, m=m),
        grid=(bsz, seq // tt),
        in_specs=[
            tile,
            _const_spec(g_cross.shape),
            pl.BlockSpec((None,) + qk.shape[1:], lambda b, t: (b, 0, 0)),
            pl.BlockSpec((None,) + vo.shape[1:], lambda b, t: (b, 0, 0)),
        ],
        out_specs=tile,
        out_shape=jax.ShapeDtypeStruct(h.shape, _F32),
        compiler_params=pltpu.CompilerParams(
            dimension_semantics=("arbitrary", "arbitrary"),
            vmem_limit_bytes=VMEM_LIMIT_BYTES),
        name="cross",
    )(h, g_cross, qk, vo)


def _shift_rows(a, first_row):
    rolled = pltpu.roll(a, 1, 0)
    row = lax.broadcasted_iota(jnp.int32, (SUBLANES, a.shape[1]), 0)
    head = jnp.where(row == 0, first_row, rolled[0:SUBLANES])
    return jnp.concatenate([head, rolled[SUBLANES:]], axis=0)


def _ffn_kernel(h_ref, gf_ref, win_ref, cw_ref, cb_ref, wd_ref, gfin_ref, o_ref,
                tail_ref, n_ref, act_ref, *, tt):
    t = pl.program_id(1)

    @pl.when(t == 0)
    def _():
        tail_ref[...] = jnp.zeros(tail_ref.shape, _F32)

    n_ref[...] = _rms(h_ref[...], gf_ref[...]).astype(_BF16)
    for c0 in range(0, D_FF, FF_COLS):
        cols = slice(c0, c0 + FF_COLS)
        u = _dot(n_ref[...], win_ref[:, cols])
        gate = _dot(n_ref[...], win_ref[:, D_FF + c0:D_FF + c0 + FF_COLS])
        w0, w1, w2 = cw_ref[0:1, cols], cw_ref[1:2, cols], cw_ref[2:3, cols]
        prev1 = tail_ref[SUBLANES - 1:SUBLANES, cols]
        prev2 = tail_ref[SUBLANES - 2:SUBLANES - 1, cols]
        acc = w1 * u + _shift_rows(w0 * u, w0 * prev1)
        conv = w2 * u + _shift_rows(acc, w1 * prev1 + w0 * prev2) + cb_ref[:, cols]
        tail_ref[:, cols] = u[tt - SUBLANES:tt, :]
        act_ref[:, cols] = (jax.nn.silu(conv) * gate).astype(_BF16)
    for r0 in range(0, tt, FF_OUT_ROWS):
        rows = slice(r0, r0 + FF_OUT_ROWS)
        h3 = h_ref[rows, :] + _dot(act_ref[rows, :], wd_ref[...])
        o_ref[rows, :] = _rms(h3, gfin_ref[...])


def _ffn(h, g_ffn, w_ffn_in, conv_w, conv_b, w_down, g_final, tt):
    bsz, seq, d = h.shape
    tile = pl.BlockSpec((None, tt, d), lambda b, t: (b, t, 0))
    return pl.pallas_call(
        functools.partial(_ffn_kernel, tt=tt),
        grid=(bsz, seq // tt),
        in_specs=[
            tile,
            _const_spec(g_ffn.shape),
            _const_spec(w_ffn_in.shape),
            _const_spec(conv_w.shape),
            _const_spec(conv_b.shape),
            _const_spec(w_down.shape),
            _const_spec(g_final.shape),
        ],
        out_specs=tile,
        out_shape=jax.ShapeDtypeStruct(h.shape, _F32),
        scratch_shapes=[
            pltpu.VMEM((SUBLANES, D_FF), _F32),
            pltpu.VMEM((tt, d), _BF16),
            pltpu.VMEM((tt, D_FF), _BF16),
        ],
        compiler_params=pltpu.CompilerParams(
            dimension_semantics=("arbitrary", "arbitrary"),
            vmem_limit_bytes=VMEM_LIMIT_BYTES),
        name="ffn",
    )(h, g_ffn, w_ffn_in, conv_w, conv_b, w_down, g_final)


def kernel(x, mem, g_mix, w_in, lower_bounds, attn_sinks, g_onorm, w_branch_a, w_branch_b,
           w_mix_out, g_cross, g_mem, w_cq, w_ckv, w_co, g_ffn, w_ffn_in, conv_w, conv_b,
           w_ffn_down, g_final):
    depth = g_mix.shape[0]
    assert depth == 1 and x.shape[-1] == D_MODEL and x.shape[1] % TOKEN_TILE == 0
    tt = TOKEN_TILE
    h = x
    for l in range(depth):
        qk, vo, (in16, a16, b16, mix16) = _mem_qk_vo(
            mem, g_mem[l][None], w_ckv[l], w_cq[l], w_co[l],
            [w_in[l], w_branch_a[l], w_branch_b[l], w_mix_out[l]])
        h, (ffn_in16, ffn_down16) = _mixer(
            h, g_mix[l][None], in16, lower_bounds.astype(_F32), attn_sinks[l], g_onorm[l][None],
            a16, b16, mix16, [w_ffn_in[l], w_ffn_down[l]], tt)
        h = _cross(h, g_cross[l][None], qk, vo, CROSS_TILE)
        h = _ffn(h, g_ffn[l][None], ffn_in16, conv_w[l], conv_b[l][None], ffn_down16,
                 g_final[None], FFN_TILE)
    return h
```

```python
import functools

import jax
import jax.numpy as jnp
from jax import lax
from jax.experimental import pallas as pl
from jax.experimental.pallas import tpu as pltpu

D_MODEL = 1024
CHUNK = 64
CHUNK_LOG2 = CHUNK.bit_length() - 1
EPS = 1e-6
LOG2_E = 1.4426950408889634

ATT_HEADS = 8
ATT_KV_HEADS = 2
ATT_HEAD_DIM = 64
ATT_GROUP = ATT_HEADS // ATT_KV_HEADS
LOOKBACK = 2
ATT_Q_W = ATT_HEADS * ATT_HEAD_DIM
ATT_KV_W = ATT_KV_HEADS * ATT_HEAD_DIM
KV_HALO = LOOKBACK * CHUNK
ATT_QBLK = 2 * CHUNK
ATT_KBLK = ATT_QBLK + KV_HALO

HG_HEADS = 4
HG_DIM = 128
HG_W = HG_HEADS * HG_DIM
HG_BLK = 4 * CHUNK

X_HEADS = 4
X_HEAD_DIM = D_MODEL // X_HEADS

D_FF = 2816
CONV_WIDTH = 3
FF_COLS = 256
FF_OUT_ROWS = 256
SUBLANES = 8
BF16_ROWS = 16

_OFF_AQ = 0
_OFF_AK = _OFF_AQ + ATT_Q_W
_OFF_AV = _OFF_AK + ATT_KV_W
_OFF_HQ = _OFF_AV + ATT_KV_W
_OFF_HF = _OFF_HQ + HG_W
_OFF_HI = _OFF_HF + HG_W
_OFF_HG = _OFF_HI + HG_W
_OFF_GA = _OFF_HG + HG_W
_OFF_GB = _OFF_GA + D_MODEL
IN_W = _OFF_GB + D_MODEL

TOKEN_TILE = 512
CROSS_TILE = 1024
FFN_TILE = 1024
VMEM_LIMIT_BYTES = 56 * 1024 * 1024

_F32 = jnp.float32
_BF16 = jnp.bfloat16
_NT = (((1,), (1,)), ((), ()))


def _rms(xf, gain):
    return xf * lax.rsqrt(jnp.mean(xf * xf, axis=-1, keepdims=True) + EPS) * gain


def _sigmoid(x):
    return 0.5 * jnp.tanh(0.5 * x) + 0.5


def _dot(a, b):
    return jnp.dot(a, b, preferred_element_type=_F32)


def _dot_nt(a, b):
    return lax.dot_general(a, b, _NT, preferred_element_type=_F32)


def _chunk_cumsum(a):
    rows, cols = a.shape
    sub = lax.broadcasted_iota(jnp.int32, (SUBLANES, cols), 0)
    groups = []
    for g0 in range(0, rows, SUBLANES):
        g = a[g0:g0 + SUBLANES, :]
        step = 1
        while step < SUBLANES:
            g = g + jnp.where(sub >= step, pltpu.roll(g, step, 0), 0.0)
            step *= 2
        if g0 % CHUNK:
            g = g + carry
        carry = jnp.broadcast_to(g[SUBLANES - 1:SUBLANES, :], (SUBLANES, cols))
        groups.append(g)
    return jnp.concatenate(groups, axis=0)


def _mixer_kernel(x_ref, gmix_ref, win_ref, lbraw_ref, sinks_ref, gon_ref, wa_ref, wb_ref,
                  wmix_ref, *rest, tt, n_later):
    later_f32 = rest[:n_later]
    o_ref = rest[n_later]
    later_bf16 = rest[n_later + 1:2 * n_later + 1]
    n_ref, proj_ref, kz_ref, vz_ref, st_ref, ao_ref, hy_ref = rest[2 * n_later + 1:]
    t = pl.program_id(1)
    win = KV_HALO + tt

    @pl.when(t == 0)
    def _():
        for ref in (kz_ref, vz_ref):
            ref[:, :, 0:KV_HALO, :] = jnp.zeros((ATT_KV_HEADS, 2, KV_HALO, ATT_KV_W), _BF16)
        st_ref[...] = jnp.zeros(st_ref.shape, _F32)

    n_ref[...] = _rms(x_ref[...], gmix_ref[...]).astype(_BF16)
    proj_ref[:, 0:_OFF_HQ] = _dot(n_ref[...], win_ref[:, 0:_OFF_HQ])
    proj_ref[:, _OFF_HQ:_OFF_GA] = _dot(n_ref[...], win_ref[:, _OFF_HQ:_OFF_GA])

    new = slice(KV_HALO, win)
    kvlane = lax.broadcasted_iota(jnp.int32, (tt, ATT_KV_W), 1)
    halves = (kvlane < ATT_HEAD_DIM, kvlane >= ATT_HEAD_DIM)
    for dst_ref, off in ((kz_ref, _OFF_AK), (vz_ref, _OFF_AV)):
        nat = proj_ref[:, off:off + ATT_KV_W]
        swapped = pltpu.roll(nat, ATT_HEAD_DIM, 1)
        for g in range(ATT_KV_HEADS):
            for par in range(2):
                dst_ref[g, par, new, :] = jnp.where(
                    halves[par], nat if g == par else swapped, 0.0).astype(_BF16)
    qi = lax.broadcasted_iota(jnp.int32, (ATT_QBLK, ATT_KBLK), 0)
    kj = lax.broadcasted_iota(jnp.int32, (ATT_QBLK, ATT_KBLK), 1)
    qc = lax.shift_right_logical(qi, CHUNK_LOG2)
    kc = lax.shift_right_logical(kj, CHUNK_LOG2)
    band = (kc >= qc) & (kc <= qc + LOOKBACK)
    absdist = jnp.abs(qi + KV_HALO - kj).astype(_F32)
    slopes = [2.0 ** (-8.0 * (h + 1) / ATT_HEADS) for h in range(ATT_HEADS)]
    alibi = [jnp.where(band, -(LOG2_E * slope) * absdist, -jnp.inf) for slope in slopes]
    q16 = (proj_ref[:, _OFF_AQ:_OFF_AQ + ATT_Q_W]
           * (LOG2_E * ATT_HEAD_DIM ** -0.5)).astype(_BF16)
    nblk = tt // ATT_QBLK
    probs = [[None] * ATT_HEADS for _ in range(nblk)]
    inv_den = [[None] * ATT_HEADS for _ in range(nblk)]
    for blk in range(nblk):
        r0 = blk * ATT_QBLK
        bias = ([jnp.where((t * tt + kj) >= KV_HALO, a, -jnp.inf) for a in alibi]
                if blk == 0 else alibi)
        for g in range(ATT_KV_HEADS):
            q2 = jnp.concatenate(
                [q16[r0:r0 + ATT_QBLK, 2 * ATT_HEAD_DIM * pair:2 * ATT_HEAD_DIM * (pair + 1)]
                 for pair in (2 * g, 2 * g + 1)], axis=0)
            for par in range(2):
                s2 = _dot_nt(q2, kz_ref[g, par, r0:r0 + ATT_KBLK, :])
                for half, h in enumerate((ATT_GROUP * g + par, ATT_GROUP * g + par + 2)):
                    s = s2[half * ATT_QBLK:(half + 1) * ATT_QBLK] + bias[h]
                    sink = sinks_ref[h] * LOG2_E
                    m = jnp.maximum(jnp.max(s, axis=-1, keepdims=True), sink)
                    p = jnp.exp2(s - m)
                    inv_den[blk][h] = 1.0 / (
                        jnp.sum(p, axis=-1, keepdims=True) + jnp.exp2(sink - m))
                    probs[blk][h] = p.astype(_BF16)

    proj_ref[:, _OFF_GA:IN_W] = _dot(n_ref[...], win_ref[:, _OFF_GA:IN_W])

    lraw = lbraw_ref[...]
    lexp = jnp.exp(lraw - jnp.max(lraw, axis=0, keepdims=True))
    lb = lexp[0:1, :] / jnp.sum(lexp, axis=0, keepdims=True)
    ri = lax.broadcasted_iota(jnp.int32, (HG_BLK, HG_BLK), 0)
    ci = lax.broadcasted_iota(jnp.int32, (HG_BLK, HG_BLK), 1)
    chunk_causal = ((lax.shift_right_logical(ri, CHUNK_LOG2)
                     == lax.shift_right_logical(ci, CHUNK_LOG2)) & (ri >= ci))
    rowchunk = lax.shift_right_logical(
        lax.broadcasted_iota(jnp.int32, (HG_BLK, HG_DIM), 0), CHUNK_LOG2)
    zero_hd = jnp.zeros((HG_BLK, HG_DIM), _BF16)
    ncb = HG_BLK // CHUNK
    hg_rows = [slice(r0, r0 + HG_BLK) for r0 in range(0, tt, HG_BLK)]

    def chunk_blocks(a):
        return jnp.concatenate([jnp.where(rowchunk == c, a, zero_hd) for c in range(ncb)], axis=1)

    def decayed_operands(rows):
        f = lb + (1.0 - lb) * _sigmoid(proj_ref[rows, _OFF_HF:_OFF_HF + HG_W])
        b = _chunk_cumsum(jnp.log2(f))
        b_last = [b[(c + 1) * CHUNK - 1:(c + 1) * CHUNK, :] for c in range(ncb)]
        b_last_rows = jnp.concatenate(
            [jnp.broadcast_to(bl, (CHUNK, HG_W)) for bl in b_last], axis=0)
        q_dec = (proj_ref[rows, _OFF_HQ:_OFF_HQ + HG_W] * (HG_DIM ** -0.5)
                 * jnp.exp2(b)).astype(_BF16)
        k_inv = ((1.0 - f) * jnp.exp2(-b)).astype(_BF16)
        k_end = ((1.0 - f) * jnp.exp2(b_last_rows - b)).astype(_BF16)
        return q_dec, k_inv, k_end, [jnp.exp2(bl) for bl in b_last]

    operands = [decayed_operands(rows) for rows in hg_rows]

    olane = lax.broadcasted_iota(jnp.int32, (ATT_QBLK, 2 * ATT_HEAD_DIM), 1)
    for blk in range(nblk):
        r0 = blk * ATT_QBLK
        for g in range(ATT_KV_HEADS):
            h0 = ATT_GROUP * g
            o4 = (_dot(jnp.concatenate([probs[blk][h0], probs[blk][h0 + 2]], axis=0),
                       vz_ref[g, 0, r0:r0 + ATT_KBLK, :])
                  + _dot(jnp.concatenate([probs[blk][h0 + 1], probs[blk][h0 + 3]], axis=0),
                         vz_ref[g, 1, r0:r0 + ATT_KBLK, :]))
            for half in range(2):
                he, pair = h0 + 2 * half, 2 * g + half
                scale2 = jnp.where(olane < ATT_HEAD_DIM, inv_den[blk][he], inv_den[blk][he + 1])
                o2 = o4[half * ATT_QBLK:(half + 1) * ATT_QBLK]
                ao_ref[r0:r0 + ATT_QBLK, 2 * ATT_HEAD_DIM * pair:2 * ATT_HEAD_DIM * (pair + 1)] = (
                    (o2 * scale2).astype(_BF16))

    def state_free_dots(rows, q_dec, k_inv, k_end):
        a_mats, ds_all, v16 = [], [], []
        for hh in range(HG_HEADS):
            sl = slice(hh * HG_DIM, (hh + 1) * HG_DIM)
            a_mats.append(
                jnp.where(chunk_causal, _dot_nt(q_dec[:, sl], k_inv[:, sl]), 0.0).astype(_BF16))
            v = proj_ref[rows, _OFF_HI + hh * HG_DIM:_OFF_HI + (hh + 1) * HG_DIM]
            v16.append(v.astype(_BF16))
            ds_all.append(_dot(v.T.astype(_BF16), chunk_blocks(k_end[:, sl])))
        return a_mats, ds_all, v16

    local = [state_free_dots(rows, *ops[:3]) for rows, ops in zip(hg_rows, operands)]

    ya = _dot(ao_ref[...], wa_ref[...])

    gon = gon_ref[...]
    state = [st_ref[hh] for hh in range(HG_HEADS)]
    for rows, (q_dec, _, _, decay), (a_mats, ds_all, v16) in zip(hg_rows, operands, local):
        for hh in range(HG_HEADS):
            sl = slice(hh * HG_DIM, (hh + 1) * HG_DIM)
            st = state[hh]
            entering = []
            for c in range(ncb):
                entering.append(st.astype(_BF16))
                st = st * decay[c][:, sl] + ds_all[hh][:, c * HG_DIM:(c + 1) * HG_DIM]
            state[hh] = st
            o = (_dot(a_mats[hh], v16[hh])
                 + _dot_nt(chunk_blocks(q_dec[:, sl]), jnp.concatenate(entering, axis=1)))
            hg = proj_ref[rows, _OFF_HG + hh * HG_DIM:_OFF_HG + (hh + 1) * HG_DIM]
            gate = hg * _sigmoid(hg)
            y = o * lax.rsqrt(jnp.mean(o * o, axis=-1, keepdims=True) + EPS) * gon * gate
            hy_ref[rows, sl] = y.astype(_BF16)
    for hh in range(HG_HEADS):
        st_ref[hh] = state[hh]

    yb = _dot(hy_ref[...], wb_ref[...])
    y = (_sigmoid(proj_ref[:, _OFF_GA:_OFF_GA + D_MODEL]) * ya
         + _sigmoid(proj_ref[:, _OFF_GB:_OFF_GB + D_MODEL]) * yb)
    o_ref[...] = x_ref[...] + _dot(y.astype(_BF16), wmix_ref[...])

    for ref in (kz_ref, vz_ref):
        ref[:, :, 0:KV_HALO, :] = ref[:, :, tt:tt + KV_HALO, :]
    _cast_slabs(later_f32, later_bf16)


def _const_spec(shape):
    return pl.BlockSpec(shape, lambda b, t: (0,) * len(shape), pipeline_mode=pl.Buffered(1))


def _slab_spec(shape, nsteps, linear_step):
    rows, cols = shape
    slab = next(s for s in range(BF16_ROWS, rows + 1, BF16_ROWS)
                if rows % s == 0 and rows // s <= nsteps)
    nslabs = rows // slab
    return pl.BlockSpec((slab, cols), lambda *ids: (linear_step(*ids) * nslabs // nsteps, 0))


def _cast_slabs(later_f32, later_bf16):
    for src, dst in zip(later_f32, later_bf16):
        dst[...] = src[...].astype(_BF16)


def _mixer(x, g_mix, w_in, lb_raw, sinks, g_onorm, w_a, w_b, w_mix, later_weights, tt):
    bsz, seq, d = x.shape
    nt = seq // tt
    tile = pl.BlockSpec((None, tt, d), lambda b, t: (b, t, 0))
    slabs = [_slab_spec(w.shape, bsz * nt, lambda b, t: b * nt + t) for w in later_weights]
    h, *later16 = pl.pallas_call(
        functools.partial(_mixer_kernel, tt=tt, n_later=len(later_weights)),
        grid=(bsz, nt),
        in_specs=[
            tile,
            _const_spec(g_mix.shape),
            _const_spec(w_in.shape),
            _const_spec(lb_raw.shape),
            pl.BlockSpec(memory_space=pltpu.SMEM),
            _const_spec(g_onorm.shape),
            _const_spec(w_a.shape),
            _const_spec(w_b.shape),
            _const_spec(w_mix.shape),
        ] + slabs,
        out_specs=[tile] + slabs,
        out_shape=[jax.ShapeDtypeStruct(x.shape, _F32)]
        + [jax.ShapeDtypeStruct(w.shape, _BF16) for w in later_weights],
        scratch_shapes=[
            pltpu.VMEM((tt, d), _BF16),
            pltpu.VMEM((tt, IN_W), _F32),
            pltpu.VMEM((ATT_KV_HEADS, 2, KV_HALO + tt, ATT_KV_W), _BF16),
            pltpu.VMEM((ATT_KV_HEADS, 2, KV_HALO + tt, ATT_KV_W), _BF16),
            pltpu.VMEM((HG_HEADS, HG_DIM, HG_DIM), _F32),
            pltpu.VMEM((tt, ATT_Q_W), _BF16),
            pltpu.VMEM((tt, HG_W), _BF16),
        ],
        compiler_params=pltpu.CompilerParams(
            dimension_semantics=("arbitrary", "arbitrary"),
            vmem_limit_bytes=VMEM_LIMIT_BYTES),
        name="mixer",
    )(x, g_mix, w_in, lb_raw, sinks, g_onorm, w_a, w_b, w_mix, *later_weights)
    return h, later16


def _mem_qk_vo_kernel(mem_ref, gmem_ref, wkv_ref, wq_ref, wo_ref, *rest, n_later, m):
    later_f32 = rest[:n_later]
    qk_ref, vo_ref = rest[n_later:n_later + 2]
    later_bf16 = rest[n_later + 2:2 * n_later + 2]
    wkv16_ref, wq16_ref, wo16_ref = rest[2 * n_later + 2:]

    @pl.when(pl.program_id(0) == 0)
    def _():
        wkv16_ref[...] = wkv_ref[...].astype(_BF16)
        wq16_ref[...] = wq_ref[...].astype(_BF16)
        wo16_ref[...] = wo_ref[...].astype(_BF16)

    nm = _rms(mem_ref[...], gmem_ref[...]).astype(_BF16)
    kv = _dot(nm, wkv16_ref[...]).astype(_BF16)
    for hd in range(X_HEADS):
        dims = slice(hd * X_HEAD_DIM, (hd + 1) * X_HEAD_DIM)
        vdims = slice(D_MODEL + hd * X_HEAD_DIM, D_MODEL + (hd + 1) * X_HEAD_DIM)
        mems = slice(hd * m, (hd + 1) * m)
        qk_ref[:, mems] = _dot_nt(wq16_ref[:, dims], kv[:, dims]).astype(_BF16)
        vo_ref[mems, :] = _dot(kv[:, vdims], wo16_ref[dims, :]).astype(_BF16)
    _cast_slabs(later_f32, later_bf16)


def _mem_qk_vo(mem, g_mem, w_ckv, w_cq, w_co, later_weights):
    bsz, m, d = mem.shape
    slabs = [_slab_spec(w.shape, bsz, lambda b: b) for w in later_weights]
    resident = lambda w: pl.BlockSpec(w.shape, lambda b: (0, 0), pipeline_mode=pl.Buffered(1))
    qk, vo, *later16 = pl.pallas_call(
        functools.partial(_mem_qk_vo_kernel, n_later=len(later_weights), m=m),
        grid=(bsz,),
        in_specs=[
            pl.BlockSpec((None, m, d), lambda b: (b, 0, 0)),
            pl.BlockSpec(g_mem.shape, lambda b: (0, 0)),
            resident(w_ckv), resident(w_cq), resident(w_co),
        ] + slabs,
        out_specs=[pl.BlockSpec((None, d, X_HEADS * m), lambda b: (b, 0, 0)),
                   pl.BlockSpec((None, X_HEADS * m, d), lambda b: (b, 0, 0))] + slabs,
        out_shape=[jax.ShapeDtypeStruct((bsz, d, X_HEADS * m), _BF16),
                   jax.ShapeDtypeStruct((bsz, X_HEADS * m, d), _BF16)]
        + [jax.ShapeDtypeStruct(w.shape, _BF16) for w in later_weights],
        scratch_shapes=[pltpu.VMEM(w.shape, _BF16) for w in (w_ckv, w_cq, w_co)],
        compiler_params=pltpu.CompilerParams(
            dimension_semantics=("arbitrary",), vmem_limit_bytes=VMEM_LIMIT_BYTES),
        name="mem_qk_vo",
    )(mem, g_mem, w_ckv, w_cq, w_co, *later_weights)
    return qk, vo, later16


def _cross_kernel(h_ref, gc_ref, qk_ref, vo_ref, o_ref, *, m):
    h = h_ref[...]
    s_all = _dot(_rms(h, gc_ref[...]).astype(_BF16), qk_ref[...]) * (LOG2_E * X_HEAD_DIM ** -0.5)
    probs = []
    for hd in range(X_HEADS):
        s = s_all[:, hd * m:(hd + 1) * m]
        p = jnp.exp2(s - jnp.max(s, axis=-1, keepdims=True))
        probs.append((p * (1.0 / jnp.sum(p, axis=-1, keepdims=True))).astype(_BF16))
    o_ref[...] = h + _dot(jnp.concatenate(probs, axis=-1), vo_ref[...])


def _cross(h, g_cross, qk, vo, tt):
    bsz, seq, d = h.shape
    m = vo.shape[1] // X_HEADS
    tile = pl.BlockSpec((None, tt, d), lambda b, t: (b, t, 0))
    return pl.pallas_call(
        functools.partial(_cross_kernel, m=m),
        grid=(bsz, seq // tt),
        in_specs=[
            tile,
            _const_spec(g_cross.shape),
            pl.BlockSpec((None,) + qk.shape[1:], lambda b, t: (b, 0, 0)),
            pl.BlockSpec((None,) + vo.shape[1:], lambda b, t: (b, 0, 0)),
        ],
        out_specs=tile,
        out_shape=jax.ShapeDtypeStruct(h.shape, _F32),
        compiler_params=pltpu.CompilerParams(
            dimension_semantics=("arbitrary", "arbitrary"),
            vmem_limit_bytes=VMEM_LIMIT_BYTES),
        name="cross",
    )(h, g_cross, qk, vo)


def _shift_rows(a, first_row):
    rolled = pltpu.roll(a, 1, 0)
    row = lax.broadcasted_iota(jnp.int32, (SUBLANES, a.shape[1]), 0)
    head = jnp.where(row == 0, first_row, rolled[0:SUBLANES])
    return jnp.concatenate([head, rolled[SUBLANES:]], axis=0)


def _ffn_kernel(h_ref, gf_ref, win_ref, cw_ref, cb_ref, wd_ref, gfin_ref, o_ref,
                tail_ref, n_ref, act_ref, *, tt):
    t = pl.program_id(1)

    @pl.when(t == 0)
    def _():
        tail_ref[...] = jnp.zeros(tail_ref.shape, _F32)

    n_ref[...] = _rms(h_ref[...], gf_ref[...]).astype(_BF16)
    for c0 in range(0, D_FF, FF_COLS):
        cols = slice(c0, c0 + FF_COLS)
        u = _dot(n_ref[...], win_ref[:, cols])
        gate = _dot(n_ref[...], win_ref[:, D_FF + c0:D_FF + c0 + FF_COLS])
        w0, w1, w2 = cw_ref[0:1, cols], cw_ref[1:2, cols], cw_ref[2:3, cols]
        prev1 = tail_ref[SUBLANES - 1:SUBLANES, cols]
        prev2 = tail_ref[SUBLANES - 2:SUBLANES - 1, cols]
        acc = w1 * u + _shift_rows(w0 * u, w0 * prev1)
        conv = w2 * u + _shift_rows(acc, w1 * prev1 + w0 * prev2) + cb_ref[:, cols]
        tail_ref[:, cols] = u[tt - SUBLANES:tt, :]
        act_ref[:, cols] = (jax.nn.silu(conv) * gate).astype(_BF16)
    for r0 in range(0, tt, FF_OUT_ROWS):
        rows = slice(r0, r0 + FF_OUT_ROWS)
        h3 = h_ref[rows, :] + _dot(act_ref[rows, :], wd_ref[...])
        o_ref[rows, :] = _rms(h3, gfin_ref[...])


def _ffn(h, g_ffn, w_ffn_in, conv_w, conv_b, w_down, g_final, tt):
    bsz, seq, d = h.shape
    tile = pl.BlockSpec((None, tt, d), lambda b, t: (b, t, 0))
    return pl.pallas_call(
        functools.partial(_ffn_kernel, tt=tt),
        grid=(bsz, seq // tt),
        in_specs=[
            tile,
            _const_spec(g_ffn.shape),
            _const_spec(w_ffn_in.shape),
            _const_spec(conv_w.shape),
            _const_spec(conv_b.shape),
            _const_spec(w_down.shape),
            _const_spec(g_final.shape),
        ],
        out_specs=tile,
        out_shape=jax.ShapeDtypeStruct(h.shape, _F32),
        scratch_shapes=[
            pltpu.VMEM((SUBLANES, D_FF), _F32),
            pltpu.VMEM((tt, d), _BF16),
            pltpu.VMEM((tt, D_FF), _BF16),
        ],
        compiler_params=pltpu.CompilerParams(
            dimension_semantics=("arbitrary", "arbitrary"),
            vmem_limit_bytes=VMEM_LIMIT_BYTES),
        name="ffn",
    )(h, g_ffn, w_ffn_in, conv_w, conv_b, w_down, g_final)


def kernel(x, mem, g_mix, w_in, lower_bounds, attn_sinks, g_onorm, w_branch_a, w_branch_b,
           w_mix_out, g_cross, g_mem, w_cq, w_ckv, w_co, g_ffn, w_ffn_in, conv_w, conv_b,
           w_ffn_down, g_final):
    depth = g_mix.shape[0]
    assert depth == 1 and x.shape[-1] == D_MODEL and x.shape[1] % TOKEN_TILE == 0
    tt = TOKEN_TILE
    h = x
    for l in range(depth):
        qk, vo, (in16, a16, b16, mix16) = _mem_qk_vo(
            mem, g_mem[l][None], w_ckv[l], w_cq[l], w_co[l],
            [w_in[l], w_branch_a[l], w_branch_b[l], w_mix_out[l]])
        h, (ffn_in16, ffn_down16) = _mixer(
            h, g_mix[l][None], in16, lower_bounds.astype(_F32), attn_sinks[l], g_onorm[l][None],
            a16, b16, mix16, [w_ffn_in[l], w_ffn_down[l]], tt)
        h = _cross(h, g_cross[l][None], qk, vo, CROSS_TILE)
        h = _ffn(h, g_ffn[l][None], ffn_in16, conv_w[l], conv_b[l][None], ffn_down16,
                 g_final[None], FFN_TILE)
    return h
```

```python
import functools

import jax
import jax.numpy as jnp
from jax import lax
from jax.experimental import pallas as pl
from jax.experimental.pallas import tpu as pltpu

D_MODEL = 1024
CHUNK = 64
CHUNK_LOG2 = CHUNK.bit_length() - 1
EPS = 1e-6
LOG2_E = 1.4426950408889634

ATT_HEADS = 8
ATT_KV_HEADS = 2
ATT_HEAD_DIM = 64
ATT_GROUP = ATT_HEADS // ATT_KV_HEADS
LOOKBACK = 2
ATT_Q_W = ATT_HEADS * ATT_HEAD_DIM
ATT_KV_W = ATT_KV_HEADS * ATT_HEAD_DIM
KV_HALO = LOOKBACK * CHUNK
ATT_QBLK = 2 * CHUNK
ATT_KBLK = ATT_QBLK + KV_HALO

HG_HEADS = 4
HG_DIM = 128
HG_W = HG_HEADS * HG_DIM
HG_BLK = 2 * CHUNK

X_HEADS = 4
X_HEAD_DIM = D_MODEL // X_HEADS

D_FF = 2816
CONV_WIDTH = 3
FF_COLS = 256
FF_OUT_ROWS = 256
SUBLANES = 8
BF16_ROWS = 16

_OFF_AQ = 0
_OFF_AK = _OFF_AQ + ATT_Q_W
_OFF_AV = _OFF_AK + ATT_KV_W
_OFF_HQ = _OFF_AV + ATT_KV_W
_OFF_HF = _OFF_HQ + HG_W
_OFF_HI = _OFF_HF + HG_W
_OFF_HG = _OFF_HI + HG_W
_OFF_GA = _OFF_HG + HG_W
_OFF_GB = _OFF_GA + D_MODEL
IN_W = _OFF_GB + D_MODEL

TOKEN_TILE = 512
CROSS_TILE = 1024
FFN_TILE = 1024
VMEM_LIMIT_BYTES = 56 * 1024 * 1024

_F32 = jnp.float32
_BF16 = jnp.bfloat16
_NT = (((1,), (1,)), ((), ()))


def _rms(xf, gain):
    return xf * lax.rsqrt(jnp.mean(xf * xf, axis=-1, keepdims=True) + EPS) * gain


def _dot(a, b):
    return jnp.dot(a, b, preferred_element_type=_F32)


def _dot_nt(a, b):
    return lax.dot_general(a, b, _NT, preferred_element_type=_F32)


def _chunk_cumsum(a):
    rows, cols = a.shape
    sub = lax.broadcasted_iota(jnp.int32, (SUBLANES, cols), 0)
    groups = []
    for g0 in range(0, rows, SUBLANES):
        g = a[g0:g0 + SUBLANES, :]
        step = 1
        while step < SUBLANES:
            g = g + jnp.where(sub >= step, pltpu.roll(g, step, 0), 0.0)
            step *= 2
        if g0 % CHUNK:
            g = g + carry
        carry = jnp.broadcast_to(g[SUBLANES - 1:SUBLANES, :], (SUBLANES, cols))
        groups.append(g)
    return jnp.concatenate(groups, axis=0)


def _mixer_kernel(x_ref, gmix_ref, win_ref, lbraw_ref, sinks_ref, gon_ref, wa_ref, wb_ref,
                  wmix_ref, *rest, tt, n_later):
    later_f32 = rest[:n_later]
    o_ref = rest[n_later]
    later_bf16 = rest[n_later + 1:2 * n_later + 1]
    n_ref, proj_ref, kz_ref, vz_ref, st_ref, ao_ref, hy_ref = rest[2 * n_later + 1:]
    t = pl.program_id(1)
    win = KV_HALO + tt

    @pl.when(t == 0)
    def _():
        for ref in (kz_ref, vz_ref):
            ref[:, :, 0:KV_HALO, :] = jnp.zeros((ATT_KV_HEADS, 2, KV_HALO, ATT_KV_W), _BF16)
        st_ref[...] = jnp.zeros(st_ref.shape, _F32)

    n_ref[...] = _rms(x_ref[...], gmix_ref[...]).astype(_BF16)
    proj_ref[:, 0:_OFF_HQ] = _dot(n_ref[...], win_ref[:, 0:_OFF_HQ])
    proj_ref[:, _OFF_HQ:_OFF_GA] = _dot(n_ref[...], win_ref[:, _OFF_HQ:_OFF_GA])

    new = slice(KV_HALO, win)
    kvlane = lax.broadcasted_iota(jnp.int32, (tt, ATT_KV_W), 1)
    halves = (kvlane < ATT_HEAD_DIM, kvlane >= ATT_HEAD_DIM)
    for dst_ref, off in ((kz_ref, _OFF_AK), (vz_ref, _OFF_AV)):
        nat = proj_ref[:, off:off + ATT_KV_W]
        swapped = pltpu.roll(nat, ATT_HEAD_DIM, 1)
        for g in range(ATT_KV_HEADS):
            for par in range(2):
                dst_ref[g, par, new, :] = jnp.where(
                    halves[par], nat if g == par else swapped, 0.0).astype(_BF16)
    qi = lax.broadcasted_iota(jnp.int32, (ATT_QBLK, ATT_KBLK), 0)
    kj = lax.broadcasted_iota(jnp.int32, (ATT_QBLK, ATT_KBLK), 1)
    qc = lax.shift_right_logical(qi, CHUNK_LOG2)
    kc = lax.shift_right_logical(kj, CHUNK_LOG2)
    band = (kc >= qc) & (kc <= qc + LOOKBACK)
    absdist = jnp.abs(qi + KV_HALO - kj).astype(_F32)
    slopes = [2.0 ** (-8.0 * (h + 1) / ATT_HEADS) for h in range(ATT_HEADS)]
    alibi = [jnp.where(band, -(LOG2_E * slope) * absdist, -jnp.inf) for slope in slopes]
    q16 = (proj_ref[:, _OFF_AQ:_OFF_AQ + ATT_Q_W]
           * (LOG2_E * ATT_HEAD_DIM ** -0.5)).astype(_BF16)
    nblk = tt // ATT_QBLK
    probs = [[None] * ATT_HEADS for _ in range(nblk)]
    inv_den = [[None] * ATT_HEADS for _ in range(nblk)]
    for blk in range(nblk):
        r0 = blk * ATT_QBLK
        bias = ([jnp.where((t * tt + kj) >= KV_HALO, a, -jnp.inf) for a in alibi]
                if blk == 0 else alibi)
        for g in range(ATT_KV_HEADS):
            q2 = jnp.concatenate(
                [q16[r0:r0 + ATT_QBLK, 2 * ATT_HEAD_DIM * pair:2 * ATT_HEAD_DIM * (pair + 1)]
                 for pair in (2 * g, 2 * g + 1)], axis=0)
            for par in range(2):
                s2 = _dot_nt(q2, kz_ref[g, par, r0:r0 + ATT_KBLK, :])
                for half, h in enumerate((ATT_GROUP * g + par, ATT_GROUP * g + par + 2)):
                    s = s2[half * ATT_QBLK:(half + 1) * ATT_QBLK] + bias[h]
                    sink = sinks_ref[h] * LOG2_E
                    m = jnp.maximum(jnp.max(s, axis=-1, keepdims=True), sink)
                    p = jnp.exp2(s - m)
                    inv_den[blk][h] = 1.0 / (
                        jnp.sum(p, axis=-1, keepdims=True) + jnp.exp2(sink - m))
                    probs[blk][h] = p.astype(_BF16)

    proj_ref[:, _OFF_GA:IN_W] = _dot(n_ref[...], win_ref[:, _OFF_GA:IN_W])

    lraw = lbraw_ref[...]
    lexp = jnp.exp(lraw - jnp.max(lraw, axis=0, keepdims=True))
    lb = lexp[0:1, :] / jnp.sum(lexp, axis=0, keepdims=True)
    ri = lax.broadcasted_iota(jnp.int32, (HG_BLK, HG_BLK), 0)
    ci = lax.broadcasted_iota(jnp.int32, (HG_BLK, HG_BLK), 1)
    chunk_causal = ((lax.shift_right_logical(ri, CHUNK_LOG2)
                     == lax.shift_right_logical(ci, CHUNK_LOG2)) & (ri >= ci))
    rowchunk = lax.shift_right_logical(
        lax.broadcasted_iota(jnp.int32, (HG_BLK, HG_DIM), 0), CHUNK_LOG2)
    zero_hd = jnp.zeros((HG_BLK, HG_DIM), _BF16)
    ncb = HG_BLK // CHUNK
    hg_rows = [slice(r0, r0 + HG_BLK) for r0 in range(0, tt, HG_BLK)]

    def chunk_blocks(a):
        return jnp.concatenate([jnp.where(rowchunk == c, a, zero_hd) for c in range(ncb)], axis=1)

    def decayed_operands(rows):
        f = lb + (1.0 - lb) * jax.nn.sigmoid(proj_ref[rows, _OFF_HF:_OFF_HF + HG_W])
        b = _chunk_cumsum(jnp.log2(f))
        b_last = [b[(c + 1) * CHUNK - 1:(c + 1) * CHUNK, :] for c in range(ncb)]
        b_last_rows = jnp.concatenate(
            [jnp.broadcast_to(bl, (CHUNK, HG_W)) for bl in b_last], axis=0)
        q_dec = (proj_ref[rows, _OFF_HQ:_OFF_HQ + HG_W] * (HG_DIM ** -0.5)
                 * jnp.exp2(b)).astype(_BF16)
        k_inv = ((1.0 - f) * jnp.exp2(-b)).astype(_BF16)
        k_end = ((1.0 - f) * jnp.exp2(b_last_rows - b)).astype(_BF16)
        return q_dec, k_inv, k_end, [jnp.exp2(bl) for bl in b_last]

    operands = [decayed_operands(rows) for rows in hg_rows]

    olane = lax.broadcasted_iota(jnp.int32, (ATT_QBLK, 2 * ATT_HEAD_DIM), 1)
    for blk in range(nblk):
        r0 = blk * ATT_QBLK
        for g in range(ATT_KV_HEADS):
            h0 = ATT_GROUP * g
            o4 = (_dot(jnp.concatenate([probs[blk][h0], probs[blk][h0 + 2]], axis=0),
                       vz_ref[g, 0, r0:r0 + ATT_KBLK, :])
                  + _dot(jnp.concatenate([probs[blk][h0 + 1], probs[blk][h0 + 3]], axis=0),
                         vz_ref[g, 1, r0:r0 + ATT_KBLK, :]))
            for half in range(2):
                he, pair = h0 + 2 * half, 2 * g + half
                scale2 = jnp.where(olane < ATT_HEAD_DIM, inv_den[blk][he], inv_den[blk][he + 1])
                o2 = o4[half * ATT_QBLK:(half + 1) * ATT_QBLK]
                ao_ref[r0:r0 + ATT_QBLK, 2 * ATT_HEAD_DIM * pair:2 * ATT_HEAD_DIM * (pair + 1)] = (
                    (o2 * scale2).astype(_BF16))

    def state_free_dots(rows, q_dec, k_inv, k_end):
        a_mats, ds_all, v16 = [], [], []
        for hh in range(HG_HEADS):
            sl = slice(hh * HG_DIM, (hh + 1) * HG_DIM)
            a_mats.append(
                jnp.where(chunk_causal, _dot_nt(q_dec[:, sl], k_inv[:, sl]), 0.0).astype(_BF16))
            v = proj_ref[rows, _OFF_HI + hh * HG_DIM:_OFF_HI + (hh + 1) * HG_DIM]
            v16.append(v.astype(_BF16))
            ds_all.append(_dot(v.T.astype(_BF16), chunk_blocks(k_end[:, sl])))
        return a_mats, ds_all, v16

    local = [state_free_dots(rows, *ops[:3]) for rows, ops in zip(hg_rows, operands)]

    ya = _dot(ao_ref[...], wa_ref[...])

    gon = gon_ref[...]
    state = [st_ref[hh] for hh in range(HG_HEADS)]
    for rows, (q_dec, _, _, decay), (a_mats, ds_all, v16) in zip(hg_rows, operands, local):
        for hh in range(HG_HEADS):
            sl = slice(hh * HG_DIM, (hh + 1) * HG_DIM)
            st = state[hh]
            entering = []
            for c in range(ncb):
                entering.append(st.astype(_BF16))
                st = st * decay[c][:, sl] + ds_all[hh][:, c * HG_DIM:(c + 1) * HG_DIM]
            state[hh] = st
            o = (_dot(a_mats[hh], v16[hh])
                 + _dot_nt(chunk_blocks(q_dec[:, sl]), jnp.concatenate(entering, axis=1)))
            gate = jax.nn.silu(proj_ref[rows, _OFF_HG + hh * HG_DIM:_OFF_HG + (hh + 1) * HG_DIM])
            y = o * lax.rsqrt(jnp.mean(o * o, axis=-1, keepdims=True) + EPS) * gon * gate
            hy_ref[rows, sl] = y.astype(_BF16)
    for hh in range(HG_HEADS):
        st_ref[hh] = state[hh]

    yb = _dot(hy_ref[...], wb_ref[...])
    y = (jax.nn.sigmoid(proj_ref[:, _OFF_GA:_OFF_GA + D_MODEL]) * ya
         + jax.nn.sigmoid(proj_ref[:, _OFF_GB:_OFF_GB + D_MODEL]) * yb)
    o_ref[...] = x_ref[...] + _dot(y.astype(_BF16), wmix_ref[...])

    for ref in (kz_ref, vz_ref):
        ref[:, :, 0:KV_HALO, :] = ref[:, :, tt:tt + KV_HALO, :]
    _cast_slabs(later_f32, later_bf16)


def _const_spec(shape):
    return pl.BlockSpec(shape, lambda b, t: (0,) * len(shape), pipeline_mode=pl.Buffered(1))


def _slab_spec(shape, nsteps, linear_step):
    rows, cols = shape
    slab = next(s for s in range(BF16_ROWS, rows + 1, BF16_ROWS)
                if rows % s == 0 and rows // s <= nsteps)
    nslabs = rows // slab
    return pl.BlockSpec((slab, cols), lambda *ids: (linear_step(*ids) * nslabs // nsteps, 0))


def _cast_slabs(later_f32, later_bf16):
    for src, dst in zip(later_f32, later_bf16):
        dst[...] = src[...].astype(_BF16)


def _mixer(x, g_mix, w_in, lb_raw, sinks, g_onorm, w_a, w_b, w_mix, later_weights, tt):
    bsz, seq, d = x.shape
    nt = seq // tt
    tile = pl.BlockSpec((None, tt, d), lambda b, t: (b, t, 0))
    slabs = [_slab_spec(w.shape, bsz * nt, lambda b, t: b * nt + t) for w in later_weights]
    h, *later16 = pl.pallas_call(
        functools.partial(_mixer_kernel, tt=tt, n_later=len(later_weights)),
        grid=(bsz, nt),
        in_specs=[
            tile,
            _const_spec(g_mix.shape),
            _const_spec(w_in.shape),
            _const_spec(lb_raw.shape),
            pl.BlockSpec(memory_space=pltpu.SMEM),
            _const_spec(g_onorm.shape),
            _const_spec(w_a.shape),
            _const_spec(w_b.shape),
            _const_spec(w_mix.shape),
        ] + slabs,
        out_specs=[tile] + slabs,
        out_shape=[jax.ShapeDtypeStruct(x.shape, _F32)]
        + [jax.ShapeDtypeStruct(w.shape, _BF16) for w in later_weights],
        scratch_shapes=[
            pltpu.VMEM((tt, d), _BF16),
            pltpu.VMEM((tt, IN_W), _F32),
            pltpu.VMEM((ATT_KV_HEADS, 2, KV_HALO + tt, ATT_KV_W), _BF16),
            pltpu.VMEM((ATT_KV_HEADS, 2, KV_HALO + tt, ATT_KV_W), _BF16),
            pltpu.VMEM((HG_HEADS, HG_DIM, HG_DIM), _F32),
            pltpu.VMEM((tt, ATT_Q_W), _BF16),
            pltpu.VMEM((tt, HG_W), _BF16),
        ],
        compiler_params=pltpu.CompilerParams(
            dimension_semantics=("arbitrary", "arbitrary"),
            vmem_limit_bytes=VMEM_LIMIT_BYTES),
        name="mixer",
    )(x, g_mix, w_in, lb_raw, sinks, g_onorm, w_a, w_b, w_mix, *later_weights)
    return h, later16


def _mem_qk_vo_kernel(mem_ref, gmem_ref, wkv_ref, wq_ref, wo_ref, *rest, n_later, m):
    later_f32 = rest[:n_later]
    qk_ref, vo_ref = rest[n_later:n_later + 2]
    later_bf16 = rest[n_later + 2:2 * n_later + 2]
    wkv16_ref, wq16_ref, wo16_ref = rest[2 * n_later + 2:]

    @pl.when(pl.program_id(0) == 0)
    def _():
        wkv16_ref[...] = wkv_ref[...].astype(_BF16)
        wq16_ref[...] = wq_ref[...].astype(_BF16)
        wo16_ref[...] = wo_ref[...].astype(_BF16)

    nm = _rms(mem_ref[...], gmem_ref[...]).astype(_BF16)
    kv = _dot(nm, wkv16_ref[...]).astype(_BF16)
    for hd in range(X_HEADS):
        dims = slice(hd * X_HEAD_DIM, (hd + 1) * X_HEAD_DIM)
        vdims = slice(D_MODEL + hd * X_HEAD_DIM, D_MODEL + (hd + 1) * X_HEAD_DIM)
        mems = slice(hd * m, (hd + 1) * m)
        qk_ref[:, mems] = _dot_nt(wq16_ref[:, dims], kv[:, dims]).astype(_BF16)
        vo_ref[mems, :] = _dot(kv[:, vdims], wo16_ref[dims, :]).astype(_BF16)
    _cast_slabs(later_f32, later_bf16)


def _mem_qk_vo(mem, g_mem, w_ckv, w_cq, w_co, later_weights):
    bsz, m, d = mem.shape
    slabs = [_slab_spec(w.shape, bsz, lambda b: b) for w in later_weights]
    resident = lambda w: pl.BlockSpec(w.shape, lambda b: (0, 0), pipeline_mode=pl.Buffered(1))
    qk, vo, *later16 = pl.pallas_call(
        functools.partial(_mem_qk_vo_kernel, n_later=len(later_weights), m=m),
        grid=(bsz,),
        in_specs=[
            pl.BlockSpec((None, m, d), lambda b: (b, 0, 0)),
            pl.BlockSpec(g_mem.shape, lambda b: (0, 0)),
            resident(w_ckv), resident(w_cq), resident(w_co),
        ] + slabs,
        out_specs=[pl.BlockSpec((None, d, X_HEADS * m), lambda b: (b, 0, 0)),
                   pl.BlockSpec((None, X_HEADS * m, d), lambda b: (b, 0, 0))] + slabs,
        out_shape=[jax.ShapeDtypeStruct((bsz, d, X_HEADS * m), _BF16),
                   jax.ShapeDtypeStruct((bsz, X_HEADS * m, d), _BF16)]
        + [jax.ShapeDtypeStruct(w.shape, _BF16) for w in later_weights],
        scratch_shapes=[pltpu.VMEM(w.shape, _BF16) for w in (w_ckv, w_cq, w_co)],
        compiler_params=pltpu.CompilerParams(
            dimension_semantics=("arbitrary",), vmem_limit_bytes=VMEM_LIMIT_BYTES),
        name="mem_qk_vo",
    )(mem, g_mem, w_ckv, w_cq, w_co, *later_weights)
    return qk, vo, later16


def _cross_kernel(h_ref, gc_ref, qk_ref, vo_ref, o_ref, *, m):
    h = h_ref[...]
    s_all = _dot(_rms(h, gc_ref[...]).astype(_BF16), qk_ref[...]) * (LOG2_E * X_HEAD_DIM ** -0.5)
    probs = []
    for hd in range(X_HEADS):
        s = s_all[:, hd * m:(hd + 1) * m]
        p = jnp.exp2(s - jnp.max(s, axis=-1, keepdims=True))
        probs.append((p * (1.0 / jnp.sum(p, axis=-1, keepdims=True))).astype(_BF16))
    o_ref[...] = h + _dot(jnp.concatenate(probs, axis=-1), vo_ref[...])


def _cross(h, g_cross, qk, vo, tt):
    bsz, seq, d = h.shape
    m = vo.shape[1] // X_HEADS
    tile = pl.BlockSpec((None, tt, d), lambda b, t: (b, t, 0))
    return pl.pallas_call(
        functools.partial(_cross_kernel, m=m),
        grid=(bsz, seq // tt),
        in_specs=[
            tile,
            _const_spec(g_cross.shape),
            pl.BlockSpec((None,) + qk.shape[1:], lambda b, t: (b, 0, 0)),
            pl.BlockSpec((None,) + vo.shape[1:], lambda b, t: (b, 0, 0)),
        ],
        out_specs=tile,
        out_shape=jax.ShapeDtypeStruct(h.shape, _F32),
        compiler_params=pltpu.CompilerParams(
            dimension_semantics=("arbitrary", "arbitrary"),
            vmem_limit_bytes=VMEM_LIMIT_BYTES),
        name="cross",
    )(h, g_cross, qk, vo)


def _shift_rows(a, first_row):
    rolled = pltpu.roll(a, 1, 0)
    row = lax.broadcasted_iota(jnp.int32, (SUBLANES, a.shape[1]), 0)
    head = jnp.where(row == 0, first_row, rolled[0:SUBLANES])
    return jnp.concatenate([head, rolled[SUBLANES:]], axis=0)


def _ffn_kernel(h_ref, gf_ref, win_ref, cw_ref, cb_ref, wd_ref, gfin_ref, o_ref,
                tail_ref, n_ref, act_ref, *, tt):
    t = pl.program_id(1)

    @pl.when(t == 0)
    def _():
        tail_ref[...] = jnp.zeros(tail_ref.shape, _F32)

    n_ref[...] = _rms(h_ref[...], gf_ref[...]).astype(_BF16)
    for c0 in range(0, D_FF, FF_COLS):
        cols = slice(c0, c0 + FF_COLS)
        u = _dot(n_ref[...], win_ref[:, cols])
        gate = _dot(n_ref[...], win_ref[:, D_FF + c0:D_FF + c0 + FF_COLS])
        w0, w1, w2 = cw_ref[0:1, cols], cw_ref[1:2, cols], cw_ref[2:3, cols]
        prev1 = tail_ref[SUBLANES - 1:SUBLANES, cols]
        prev2 = tail_ref[SUBLANES - 2:SUBLANES - 1, cols]
        acc = w1 * u + _shift_rows(w0 * u, w0 * prev1)
        conv = w2 * u + _shift_rows(acc, w1 * prev1 + w0 * prev2) + cb_ref[:, cols]
        tail_ref[:, cols] = u[tt - SUBLANES:tt, :]
        act_ref[:, cols] = (jax.nn.silu(conv) * gate).astype(_BF16)
    for r0 in range(0, tt, FF_OUT_ROWS):
        rows = slice(r0, r0 + FF_OUT_ROWS)
        h3 = h_ref[rows, :] + _dot(act_ref[rows, :], wd_ref[...])
        o_ref[rows, :] = _rms(h3, gfin_ref[...])


def _ffn(h, g_ffn, w_ffn_in, conv_w, conv_b, w_down, g_final, tt):
    bsz, seq, d = h.shape
    tile = pl.BlockSpec((None, tt, d), lambda b, t: (b, t, 0))
    return pl.pallas_call(
        functools.partial(_ffn_kernel, tt=tt),
        grid=(bsz, seq // tt),
        in_specs=[
            tile,
            _const_spec(g_ffn.shape),
            _const_spec(w_ffn_in.shape),
            _const_spec(conv_w.shape),
            _const_spec(conv_b.shape),
            _const_spec(w_down.shape),
            _const_spec(g_final.shape),
        ],
        out_specs=tile,
        out_shape=jax.ShapeDtypeStruct(h.shape, _F32),
        scratch_shapes=[
            pltpu.VMEM((SUBLANES, D_FF), _F32),
            pltpu.VMEM((tt, d), _BF16),
            pltpu.VMEM((tt, D_FF), _BF16),
        ],
        compiler_params=pltpu.CompilerParams(
            dimension_semantics=("arbitrary", "arbitrary"),
            vmem_limit_bytes=VMEM_LIMIT_BYTES),
        name="ffn",
    )(h, g_ffn, w_ffn_in, conv_w, conv_b, w_down, g_final)


def kernel(x, mem, g_mix, w_in, lower_bounds, attn_sinks, g_onorm, w_branch_a, w_branch_b,
           w_mix_out, g_cross, g_mem, w_cq, w_ckv, w_co, g_ffn, w_ffn_in, conv_w, conv_b,
           w_ffn_down, g_final):
    depth = g_mix.shape[0]
    assert depth == 1 and x.shape[-1] == D_MODEL and x.shape[1] % TOKEN_TILE == 0
    tt = TOKEN_TILE
    h = x
    for l in range(depth):
        qk, vo, (in16, a16, b16, mix16) = _mem_qk_vo(
            mem, g_mem[l][None], w_ckv[l], w_cq[l], w_co[l],
            [w_in[l], w_branch_a[l], w_branch_b[l], w_mix_out[l]])
        h, (ffn_in16, ffn_down16) = _mixer(
            h, g_mix[l][None], in16, lower_bounds.astype(_F32), attn_sinks[l], g_onorm[l][None],
            a16, b16, mix16, [w_ffn_in[l], w_ffn_down[l]], tt)
        h = _cross(h, g_cross[l][None], qk, vo, CROSS_TILE)
        h = _ffn(h, g_ffn[l][None], ffn_in16, conv_w[l], conv_b[l][None], ffn_down16,
                 g_final[None], FFN_TILE)
    return h
```

```python
import functools

import jax
import jax.numpy as jnp
from jax import lax
from jax.experimental import pallas as pl
from jax.experimental.pallas import tpu as pltpu

D_MODEL = 1024
CHUNK = 64
CHUNK_LOG2 = CHUNK.bit_length() - 1
EPS = 1e-6
LOG2_E = 1.4426950408889634

ATT_HEADS = 8
ATT_KV_HEADS = 2
ATT_HEAD_DIM = 64
ATT_GROUP = ATT_HEADS // ATT_KV_HEADS
LOOKBACK = 2
ATT_Q_W = ATT_HEADS * ATT_HEAD_DIM
ATT_KV_W = ATT_KV_HEADS * ATT_HEAD_DIM
KV_HALO = LOOKBACK * CHUNK
ATT_QBLK = 2 * CHUNK
ATT_KBLK = ATT_QBLK + KV_HALO

HG_HEADS = 4
HG_DIM = 128
HG_W = HG_HEADS * HG_DIM
HG_BLK = 2 * CHUNK

X_HEADS = 4
X_HEAD_DIM = D_MODEL // X_HEADS

D_FF = 2816
CONV_WIDTH = 3
FF_COLS = 256
FF_OUT_ROWS = 256
SUBLANES = 8
BF16_ROWS = 16

_OFF_AQ = 0
_OFF_AK = _OFF_AQ + ATT_Q_W
_OFF_AV = _OFF_AK + ATT_KV_W
_OFF_HQ = _OFF_AV + ATT_KV_W
_OFF_HF = _OFF_HQ + HG_W
_OFF_HI = _OFF_HF + HG_W
_OFF_HG = _OFF_HI + HG_W
_OFF_GA = _OFF_HG + HG_W
_OFF_GB = _OFF_GA + D_MODEL
IN_W = _OFF_GB + D_MODEL

TOKEN_TILE = 512
CROSS_TILE = 2048
FFN_TILE = 1024
VMEM_LIMIT_BYTES = 56 * 1024 * 1024

_F32 = jnp.float32
_BF16 = jnp.bfloat16
_NT = (((1,), (1,)), ((), ()))


def _rms(xf, gain):
    return xf * lax.rsqrt(jnp.mean(xf * xf, axis=-1, keepdims=True) + EPS) * gain


def _dot(a, b):
    return jnp.dot(a, b, preferred_element_type=_F32)


def _dot_nt(a, b):
    return lax.dot_general(a, b, _NT, preferred_element_type=_F32)


def _chunk_cumsum(a):
    rows, cols = a.shape
    sub = lax.broadcasted_iota(jnp.int32, (SUBLANES, cols), 0)
    groups = []
    for g0 in range(0, rows, SUBLANES):
        g = a[g0:g0 + SUBLANES, :]
        step = 1
        while step < SUBLANES:
            g = g + jnp.where(sub >= step, pltpu.roll(g, step, 0), 0.0)
            step *= 2
        if g0 % CHUNK:
            g = g + carry
        carry = jnp.broadcast_to(g[SUBLANES - 1:SUBLANES, :], (SUBLANES, cols))
        groups.append(g)
    return jnp.concatenate(groups, axis=0)


def _mixer_kernel(x_ref, gmix_ref, win_ref, lbraw_ref, sinks_ref, gon_ref, wa_ref, wb_ref,
                  wmix_ref, *rest, tt, n_later):
    later_f32 = rest[:n_later]
    o_ref = rest[n_later]
    later_bf16 = rest[n_later + 1:2 * n_later + 1]
    n_ref, proj_ref, kz_ref, vz_ref, st_ref, ao_ref, hy_ref = rest[2 * n_later + 1:]
    t = pl.program_id(1)
    win = KV_HALO + tt

    @pl.when(t == 0)
    def _():
        for ref in (kz_ref, vz_ref):
            ref[:, :, 0:KV_HALO, :] = jnp.zeros((ATT_KV_HEADS, 2, KV_HALO, ATT_KV_W), _BF16)
        st_ref[...] = jnp.zeros(st_ref.shape, _F32)

    n_ref[...] = _rms(x_ref[...], gmix_ref[...]).astype(_BF16)
    proj_ref[:, 0:_OFF_HQ] = _dot(n_ref[...], win_ref[:, 0:_OFF_HQ])
    proj_ref[:, _OFF_HQ:_OFF_GA] = _dot(n_ref[...], win_ref[:, _OFF_HQ:_OFF_GA])

    new = slice(KV_HALO, win)
    kvlane = lax.broadcasted_iota(jnp.int32, (tt, ATT_KV_W), 1)
    halves = (kvlane < ATT_HEAD_DIM, kvlane >= ATT_HEAD_DIM)
    for dst_ref, off in ((kz_ref, _OFF_AK), (vz_ref, _OFF_AV)):
        nat = proj_ref[:, off:off + ATT_KV_W]
        swapped = pltpu.roll(nat, ATT_HEAD_DIM, 1)
        for g in range(ATT_KV_HEADS):
            for par in range(2):
                dst_ref[g, par, new, :] = jnp.where(
                    halves[par], nat if g == par else swapped, 0.0).astype(_BF16)
    qi = lax.broadcasted_iota(jnp.int32, (ATT_QBLK, ATT_KBLK), 0)
    kj = lax.broadcasted_iota(jnp.int32, (ATT_QBLK, ATT_KBLK), 1)
    qc = lax.shift_right_logical(qi, CHUNK_LOG2)
    kc = lax.shift_right_logical(kj, CHUNK_LOG2)
    band = (kc >= qc) & (kc <= qc + LOOKBACK)
    absdist = jnp.abs(qi + KV_HALO - kj).astype(_F32)
    slopes = [2.0 ** (-8.0 * (h + 1) / ATT_HEADS) for h in range(ATT_HEADS)]
    alibi = [jnp.where(band, -(LOG2_E * slope) * absdist, -jnp.inf) for slope in slopes]
    q16 = (proj_ref[:, _OFF_AQ:_OFF_AQ + ATT_Q_W]
           * (LOG2_E * ATT_HEAD_DIM ** -0.5)).astype(_BF16)
    nblk = tt // ATT_QBLK
    probs = [[None] * ATT_HEADS for _ in range(nblk)]
    inv_den = [[None] * ATT_HEADS for _ in range(nblk)]
    for blk in range(nblk):
        r0 = blk * ATT_QBLK
        bias = ([jnp.where((t * tt + kj) >= KV_HALO, a, -jnp.inf) for a in alibi]
                if blk == 0 else alibi)
        for g in range(ATT_KV_HEADS):
            q2 = jnp.concatenate(
                [q16[r0:r0 + ATT_QBLK, 2 * ATT_HEAD_DIM * pair:2 * ATT_HEAD_DIM * (pair + 1)]
                 for pair in (2 * g, 2 * g + 1)], axis=0)
            for par in range(2):
                s2 = _dot_nt(q2, kz_ref[g, par, r0:r0 + ATT_KBLK, :])
                for half, h in enumerate((ATT_GROUP * g + par, ATT_GROUP * g + par + 2)):
                    s = s2[half * ATT_QBLK:(half + 1) * ATT_QBLK] + bias[h]
                    sink = sinks_ref[h] * LOG2_E
                    m = jnp.maximum(jnp.max(s, axis=-1, keepdims=True), sink)
                    p = jnp.exp2(s - m)
                    inv_den[blk][h] = 1.0 / (
                        jnp.sum(p, axis=-1, keepdims=True) + jnp.exp2(sink - m))
                    probs[blk][h] = p.astype(_BF16)

    proj_ref[:, _OFF_GA:IN_W] = _dot(n_ref[...], win_ref[:, _OFF_GA:IN_W])

    lraw = lbraw_ref[...]
    lexp = jnp.exp(lraw - jnp.max(lraw, axis=0, keepdims=True))
    lb = lexp[0:1, :] / jnp.sum(lexp, axis=0, keepdims=True)
    ri = lax.broadcasted_iota(jnp.int32, (HG_BLK, HG_BLK), 0)
    ci = lax.broadcasted_iota(jnp.int32, (HG_BLK, HG_BLK), 1)
    chunk_causal = ((lax.shift_right_logical(ri, CHUNK_LOG2)
                     == lax.shift_right_logical(ci, CHUNK_LOG2)) & (ri >= ci))
    rowchunk = lax.shift_right_logical(
        lax.broadcasted_iota(jnp.int32, (HG_BLK, HG_DIM), 0), CHUNK_LOG2)
    zero_hd = jnp.zeros((HG_BLK, HG_DIM), _BF16)
    ncb = HG_BLK // CHUNK
    hg_rows = [slice(r0, r0 + HG_BLK) for r0 in range(0, tt, HG_BLK)]

    def chunk_blocks(a):
        return jnp.concatenate([jnp.where(rowchunk == c, a, zero_hd) for c in range(ncb)], axis=1)

    def decayed_operands(rows):
        f = lb + (1.0 - lb) * jax.nn.sigmoid(proj_ref[rows, _OFF_HF:_OFF_HF + HG_W])
        b = _chunk_cumsum(jnp.log2(f))
        b_last = [b[(c + 1) * CHUNK - 1:(c + 1) * CHUNK, :] for c in range(ncb)]
        b_last_rows = jnp.concatenate(
            [jnp.broadcast_to(bl, (CHUNK, HG_W)) for bl in b_last], axis=0)
        q_dec = (proj_ref[rows, _OFF_HQ:_OFF_HQ + HG_W] * (HG_DIM ** -0.5)
                 * jnp.exp2(b)).astype(_BF16)
        k_inv = ((1.0 - f) * jnp.exp2(-b)).astype(_BF16)
        k_end = ((1.0 - f) * jnp.exp2(b_last_rows - b)).astype(_BF16)
        return q_dec, k_inv, k_end, [jnp.exp2(bl) for bl in b_last]

    operands = [decayed_operands(rows) for rows in hg_rows]

    olane = lax.broadcasted_iota(jnp.int32, (ATT_QBLK, 2 * ATT_HEAD_DIM), 1)
    for blk in range(nblk):
        r0 = blk * ATT_QBLK
        for g in range(ATT_KV_HEADS):
            h0 = ATT_GROUP * g
            o4 = (_dot(jnp.concatenate([probs[blk][h0], probs[blk][h0 + 2]], axis=0),
                       vz_ref[g, 0, r0:r0 + ATT_KBLK, :])
                  + _dot(jnp.concatenate([probs[blk][h0 + 1], probs[blk][h0 + 3]], axis=0),
                         vz_ref[g, 1, r0:r0 + ATT_KBLK, :]))
            for half in range(2):
                he, pair = h0 + 2 * half, 2 * g + half
                scale2 = jnp.where(olane < ATT_HEAD_DIM, inv_den[blk][he], inv_den[blk][he + 1])
                o2 = o4[half * ATT_QBLK:(half + 1) * ATT_QBLK]
                ao_ref[r0:r0 + ATT_QBLK, 2 * ATT_HEAD_DIM * pair:2 * ATT_HEAD_DIM * (pair + 1)] = (
                    (o2 * scale2).astype(_BF16))

    def state_free_dots(rows, q_dec, k_inv, k_end):
        a_mats, ds_all, v16 = [], [], []
        for hh in range(HG_HEADS):
            sl = slice(hh * HG_DIM, (hh + 1) * HG_DIM)
            a_mats.append(
                jnp.where(chunk_causal, _dot_nt(q_dec[:, sl], k_inv[:, sl]), 0.0).astype(_BF16))
            v = proj_ref[rows, _OFF_HI + hh * HG_DIM:_OFF_HI + (hh + 1) * HG_DIM]
            v16.append(v.astype(_BF16))
            ds_all.append(_dot(v.T.astype(_BF16), chunk_blocks(k_end[:, sl])))
        return a_mats, ds_all, v16

    local = [state_free_dots(rows, *ops[:3]) for rows, ops in zip(hg_rows, operands)]

    ya = _dot(ao_ref[...], wa_ref[...])

    gon = gon_ref[...]
    state = [st_ref[hh] for hh in range(HG_HEADS)]
    for rows, (q_dec, _, _, decay), (a_mats, ds_all, v16) in zip(hg_rows, operands, local):
        for hh in range(HG_HEADS):
            sl = slice(hh * HG_DIM, (hh + 1) * HG_DIM)
            st = state[hh]
            entering = []
            for c in range(ncb):
                entering.append(st.astype(_BF16))
                st = st * decay[c][:, sl] + ds_all[hh][:, c * HG_DIM:(c + 1) * HG_DIM]
            state[hh] = st
            o = (_dot(a_mats[hh], v16[hh])
                 + _dot_nt(chunk_blocks(q_dec[:, sl]), jnp.concatenate(entering, axis=1)))
            gate = jax.nn.silu(proj_ref[rows, _OFF_HG + hh * HG_DIM:_OFF_HG + (hh + 1) * HG_DIM])
            y = o * lax.rsqrt(jnp.mean(o * o, axis=-1, keepdims=True) + EPS) * gon * gate
            hy_ref[rows, sl] = y.astype(_BF16)
    for hh in range(HG_HEADS):
        st_ref[hh] = state[hh]

    yb = _dot(hy_ref[...], wb_ref[...])
    y = (jax.nn.sigmoid(proj_ref[:, _OFF_GA:_OFF_GA + D_MODEL]) * ya
         + jax.nn.sigmoid(proj_ref[:, _OFF_GB:_OFF_GB + D_MODEL]) * yb)
    o_ref[...] = x_ref[...] + _dot(y.astype(_BF16), wmix_ref[...])

    for ref in (kz_ref, vz_ref):
        ref[:, :, 0:KV_HALO, :] = ref[:, :, tt:tt + KV_HALO, :]
    _cast_slabs(later_f32, later_bf16)


def _const_spec(shape):
    return pl.BlockSpec(shape, lambda b, t: (0,) * len(shape), pipeline_mode=pl.Buffered(1))


def _slab_spec(shape, nsteps, linear_step):
    rows, cols = shape
    slab = next(s for s in range(BF16_ROWS, rows + 1, BF16_ROWS)
                if rows % s == 0 and rows // s <= nsteps)
    nslabs = rows // slab
    return pl.BlockSpec((slab, cols), lambda *ids: (linear_step(*ids) * nslabs // nsteps, 0))


def _cast_slabs(later_f32, later_bf16):
    for src, dst in zip(later_f32, later_bf16):
        dst[...] = src[...].astype(_BF16)


def _mixer(x, g_mix, w_in, lb_raw, sinks, g_onorm, w_a, w_b, w_mix, later_weights, tt):
    bsz, seq, d = x.shape
    nt = seq // tt
    tile = pl.BlockSpec((None, tt, d), lambda b, t: (b, t, 0))
    slabs = [_slab_spec(w.shape, bsz * nt, lambda b, t: b * nt + t) for w in later_weights]
    h, *later16 = pl.pallas_call(
        functools.partial(_mixer_kernel, tt=tt, n_later=len(later_weights)),
        grid=(bsz, nt),
        in_specs=[
            tile,
            _const_spec(g_mix.shape),
            _const_spec(w_in.shape),
            _const_spec(lb_raw.shape),
            pl.BlockSpec(memory_space=pltpu.SMEM),
            _const_spec(g_onorm.shape),
            _const_spec(w_a.shape),
            _const_spec(w_b.shape),
            _const_spec(w_mix.shape),
        ] + slabs,
        out_specs=[tile] + slabs,
        out_shape=[jax.ShapeDtypeStruct(x.shape, _F32)]
        + [jax.ShapeDtypeStruct(w.shape, _BF16) for w in later_weights],
        scratch_shapes=[
            pltpu.VMEM((tt, d), _BF16),
            pltpu.VMEM((tt, IN_W), _F32),
            pltpu.VMEM((ATT_KV_HEADS, 2, KV_HALO + tt, ATT_KV_W), _BF16),
            pltpu.VMEM((ATT_KV_HEADS, 2, KV_HALO + tt, ATT_KV_W), _BF16),
            pltpu.VMEM((HG_HEADS, HG_DIM, HG_DIM), _F32),
            pltpu.VMEM((tt, ATT_Q_W), _BF16),
            pltpu.VMEM((tt, HG_W), _BF16),
        ],
        compiler_params=pltpu.CompilerParams(
            dimension_semantics=("arbitrary", "arbitrary"),
            vmem_limit_bytes=VMEM_LIMIT_BYTES),
        name="mixer",
    )(x, g_mix, w_in, lb_raw, sinks, g_onorm, w_a, w_b, w_mix, *later_weights)
    return h, later16


def _mem_qk_vo_kernel(mem_ref, gmem_ref, wkv_ref, wq_ref, wo_ref, *rest, n_later, m):
    later_f32 = rest[:n_later]
    qk_ref, vo_ref = rest[n_later:n_later + 2]
    later_bf16 = rest[n_later + 2:2 * n_later + 2]
    wkv16_ref, wq16_ref, wo16_ref = rest[2 * n_later + 2:]

    @pl.when(pl.program_id(0) == 0)
    def _():
        wkv16_ref[...] = wkv_ref[...].astype(_BF16)
        wq16_ref[...] = wq_ref[...].astype(_BF16)
        wo16_ref[...] = wo_ref[...].astype(_BF16)

    nm = _rms(mem_ref[...], gmem_ref[...]).astype(_BF16)
    kv = _dot(nm, wkv16_ref[...]).astype(_BF16)
    for hd in range(X_HEADS):
        dims = slice(hd * X_HEAD_DIM, (hd + 1) * X_HEAD_DIM)
        vdims = slice(D_MODEL + hd * X_HEAD_DIM, D_MODEL + (hd + 1) * X_HEAD_DIM)
        mems = slice(hd * m, (hd + 1) * m)
        qk_ref[:, mems] = _dot_nt(wq16_ref[:, dims], kv[:, dims]).astype(_BF16)
        vo_ref[mems, :] = _dot(kv[:, vdims], wo16_ref[dims, :]).astype(_BF16)
    _cast_slabs(later_f32, later_bf16)


def _mem_qk_vo(mem, g_mem, w_ckv, w_cq, w_co, later_weights):
    bsz, m, d = mem.shape
    slabs = [_slab_spec(w.shape, bsz, lambda b: b) for w in later_weights]
    resident = lambda w: pl.BlockSpec(w.shape, lambda b: (0, 0), pipeline_mode=pl.Buffered(1))
    qk, vo, *later16 = pl.pallas_call(
        functools.partial(_mem_qk_vo_kernel, n_later=len(later_weights), m=m),
        grid=(bsz,),
        in_specs=[
            pl.BlockSpec((None, m, d), lambda b: (b, 0, 0)),
            pl.BlockSpec(g_mem.shape, lambda b: (0, 0)),
            resident(w_ckv), resident(w_cq), resident(w_co),
        ] + slabs,
        out_specs=[pl.BlockSpec((None, d, X_HEADS * m), lambda b: (b, 0, 0)),
                   pl.BlockSpec((None, X_HEADS * m, d), lambda b: (b, 0, 0))] + slabs,
        out_shape=[jax.ShapeDtypeStruct((bsz, d, X_HEADS * m), _BF16),
                   jax.ShapeDtypeStruct((bsz, X_HEADS * m, d), _BF16)]
        + [jax.ShapeDtypeStruct(w.shape, _BF16) for w in later_weights],
        scratch_shapes=[pltpu.VMEM(w.shape, _BF16) for w in (w_ckv, w_cq, w_co)],
        compiler_params=pltpu.CompilerParams(
            dimension_semantics=("arbitrary",), vmem_limit_bytes=VMEM_LIMIT_BYTES),
        name="mem_qk_vo",
    )(mem, g_mem, w_ckv, w_cq, w_co, *later_weights)
    return qk, vo, later16


def _cross_kernel(h_ref, gc_ref, qk_ref, vo_ref, o_ref, *, m):
    h = h_ref[...]
    s_all = _dot(_rms(h, gc_ref[...]).astype(_BF16), qk_ref[...]) * (LOG2_E * X_HEAD_DIM ** -0.5)
    probs = []
    for hd in range(X_HEADS):
        s = s_all[:, hd * m:(hd + 1) * m]
        p = jnp.exp2(s - jnp.max(s, axis=-1, keepdims=True))
        probs.append((p * (1.0 / jnp.sum(p, axis=-1, keepdims=True))).astype(_BF16))
    o_ref[...] = h + _dot(jnp.concatenate(probs, axis=-1), vo_ref[...])


def _cross(h, g_cross, qk, vo, tt):
    bsz, seq, d = h.shape
    m = vo.shape[1] // X_HEADS
    tile = pl.BlockSpec((None, tt, d), lambda b, t: (b, t, 0))
    return pl.pallas_call(
        functools.partial(_cross_kernel, m=m),
        grid=(bsz, seq // tt),
        in_specs=[
            tile,
            _const_spec(g_cross.shape),
            pl.BlockSpec((None,) + qk.shape[1:], lambda b, t: (b, 0, 0)),
            pl.BlockSpec((None,) + vo.shape[1:], lambda b, t: (b, 0, 0)),
        ],
        out_specs=tile,
        out_shape=jax.ShapeDtypeStruct(h.shape, _F32),
        compiler_params=pltpu.CompilerParams(
            dimension_semantics=("arbitrary", "arbitrary"),
            vmem_limit_bytes=VMEM_LIMIT_BYTES),
        name="cross",
    )(h, g_cross, qk, vo)


def _shift_rows(a, first_row):
    rolled = pltpu.roll(a, 1, 0)
    row = lax.broadcasted_iota(jnp.int32, (SUBLANES, a.shape[1]), 0)
    head = jnp.where(row == 0, first_row, rolled[0:SUBLANES])
    return jnp.concatenate([head, rolled[SUBLANES:]], axis=0)


def _ffn_kernel(h_ref, gf_ref, win_ref, cw_ref, cb_ref, wd_ref, gfin_ref, o_ref,
                tail_ref, n_ref, act_ref, *, tt):
    t = pl.program_id(1)

    @pl.when(t == 0)
    def _():
        tail_ref[...] = jnp.zeros(tail_ref.shape, _F32)

    n_ref[...] = _rms(h_ref[...], gf_ref[...]).astype(_BF16)
    for c0 in range(0, D_FF, FF_COLS):
        cols = slice(c0, c0 + FF_COLS)
        u = _dot(n_ref[...], win_ref[:, cols])
        gate = _dot(n_ref[...], win_ref[:, D_FF + c0:D_FF + c0 + FF_COLS])
        w0, w1, w2 = cw_ref[0:1, cols], cw_ref[1:2, cols], cw_ref[2:3, cols]
        prev1 = tail_ref[SUBLANES - 1:SUBLANES, cols]
        prev2 = tail_ref[SUBLANES - 2:SUBLANES - 1, cols]
        acc = w1 * u + _shift_rows(w0 * u, w0 * prev1)
        conv = w2 * u + _shift_rows(acc, w1 * prev1 + w0 * prev2) + cb_ref[:, cols]
        tail_ref[:, cols] = u[tt - SUBLANES:tt, :]
        act_ref[:, cols] = (jax.nn.silu(conv) * gate).astype(_BF16)
    for r0 in range(0, tt, FF_OUT_ROWS):
        rows = slice(r0, r0 + FF_OUT_ROWS)
        h3 = h_ref[rows, :] + _dot(act_ref[rows, :], wd_ref[...])
        o_ref[rows, :] = _rms(h3, gfin_ref[...])


def _ffn(h, g_ffn, w_ffn_in, conv_w, conv_b, w_down, g_final, tt):
    bsz, seq, d = h.shape
    tile = pl.BlockSpec((None, tt, d), lambda b, t: (b, t, 0))
    return pl.pallas_call(
        functools.partial(_ffn_kernel, tt=tt),
        grid=(bsz, seq // tt),
        in_specs=[
            tile,
            _const_spec(g_ffn.shape),
            _const_spec(w_ffn_in.shape),
            _const_spec(conv_w.shape),
            _const_spec(conv_b.shape),
            _const_spec(w_down.shape),
            _const_spec(g_final.shape),
        ],
        out_specs=tile,
        out_shape=jax.ShapeDtypeStruct(h.shape, _F32),
        scratch_shapes=[
            pltpu.VMEM((SUBLANES, D_FF), _F32),
            pltpu.VMEM((tt, d), _BF16),
            pltpu.VMEM((tt, D_FF), _BF16),
        ],
        compiler_params=pltpu.CompilerParams(
            dimension_semantics=("arbitrary", "arbitrary"),
            vmem_limit_bytes=VMEM_LIMIT_BYTES),
        name="ffn",
    )(h, g_ffn, w_ffn_in, conv_w, conv_b, w_down, g_final)


def kernel(x, mem, g_mix, w_in, lower_bounds, attn_sinks, g_onorm, w_branch_a, w_branch_b,
           w_mix_out, g_cross, g_mem, w_cq, w_ckv, w_co, g_ffn, w_ffn_in, conv_w, conv_b,
           w_ffn_down, g_final):
    depth = g_mix.shape[0]
    assert depth == 1 and x.shape[-1] == D_MODEL and x.shape[1] % TOKEN_TILE == 0
    tt = TOKEN_TILE
    h = x
    for l in range(depth):
        qk, vo, (in16, a16, b16, mix16) = _mem_qk_vo(
            mem, g_mem[l][None], w_ckv[l], w_cq[l], w_co[l],
            [w_in[l], w_branch_a[l], w_branch_b[l], w_mix_out[l]])
        h, (ffn_in16, ffn_down16) = _mixer(
            h, g_mix[l][None], in16, lower_bounds.astype(_F32), attn_sinks[l], g_onorm[l][None],
            a16, b16, mix16, [w_ffn_in[l], w_ffn_down[l]], tt)
        h = _cross(h, g_cross[l][None], qk, vo, CROSS_TILE)
        h = _ffn(h, g_ffn[l][None], ffn_in16, conv_w[l], conv_b[l][None], ffn_down16,
                 g_final[None], FFN_TILE)
    return h
```

```python
import functools

import jax
import jax.numpy as jnp
from jax import lax
from jax.experimental import pallas as pl
from jax.experimental.pallas import tpu as pltpu

D_MODEL = 1024
CHUNK = 64
CHUNK_LOG2 = CHUNK.bit_length() - 1
EPS = 1e-6
LOG2_E = 1.4426950408889634

ATT_HEADS = 8
ATT_KV_HEADS = 2
ATT_HEAD_DIM = 64
ATT_GROUP = ATT_HEADS // ATT_KV_HEADS
LOOKBACK = 2
ATT_Q_W = ATT_HEADS * ATT_HEAD_DIM
ATT_KV_W = ATT_KV_HEADS * ATT_HEAD_DIM
KV_HALO = LOOKBACK * CHUNK
ATT_QBLK = 2 * CHUNK
ATT_KBLK = ATT_QBLK + KV_HALO

HG_HEADS = 4
HG_DIM = 128
HG_W = HG_HEADS * HG_DIM
HG_BLK = 2 * CHUNK

X_HEADS = 4
X_HEAD_DIM = D_MODEL // X_HEADS

D_FF = 2816
CONV_WIDTH = 3
FF_COLS = 256
FF_OUT_ROWS = 256
SUBLANES = 8
BF16_ROWS = 16

_OFF_AQ = 0
_OFF_AK = _OFF_AQ + ATT_Q_W
_OFF_AV = _OFF_AK + ATT_KV_W
_OFF_HQ = _OFF_AV + ATT_KV_W
_OFF_HF = _OFF_HQ + HG_W
_OFF_HI = _OFF_HF + HG_W
_OFF_HG = _OFF_HI + HG_W
_OFF_GA = _OFF_HG + HG_W
_OFF_GB = _OFF_GA + D_MODEL
IN_W = _OFF_GB + D_MODEL

TOKEN_TILE = 512
FFN_TILE = 1024
VMEM_LIMIT_BYTES = 56 * 1024 * 1024

_F32 = jnp.float32
_BF16 = jnp.bfloat16
_NT = (((1,), (1,)), ((), ()))


def _rms(xf, gain):
    return xf * lax.rsqrt(jnp.mean(xf * xf, axis=-1, keepdims=True) + EPS) * gain


def _dot(a, b):
    return jnp.dot(a, b, preferred_element_type=_F32)


def _dot_nt(a, b):
    return lax.dot_general(a, b, _NT, preferred_element_type=_F32)


def _cross_attend(h, gain, qk_ref, vo_ref):
    m = vo_ref.shape[0] // X_HEADS
    s_all = _dot(_rms(h, gain).astype(_BF16), qk_ref[...]) * (LOG2_E * X_HEAD_DIM ** -0.5)
    probs = []
    for hd in range(X_HEADS):
        s = s_all[:, hd * m:(hd + 1) * m]
        p = jnp.exp2(s - jnp.max(s, axis=-1, keepdims=True))
        probs.append((p * (1.0 / jnp.sum(p, axis=-1, keepdims=True))).astype(_BF16))
    return h + _dot(jnp.concatenate(probs, axis=-1), vo_ref[...])


def _chunk_cumsum(a):
    rows, cols = a.shape
    sub = lax.broadcasted_iota(jnp.int32, (SUBLANES, cols), 0)
    groups = []
    for g0 in range(0, rows, SUBLANES):
        g = a[g0:g0 + SUBLANES, :]
        step = 1
        while step < SUBLANES:
            g = g + jnp.where(sub >= step, pltpu.roll(g, step, 0), 0.0)
            step *= 2
        if g0 % CHUNK:
            g = g + carry
        carry = jnp.broadcast_to(g[SUBLANES - 1:SUBLANES, :], (SUBLANES, cols))
        groups.append(g)
    return jnp.concatenate(groups, axis=0)


def _mixer_kernel(x_ref, gmix_ref, win_ref, lbraw_ref, sinks_ref, gon_ref, wa_ref, wb_ref,
                  wmix_ref, gc_ref, qk_ref, vo_ref, *rest, tt, n_later):
    later_f32 = rest[:n_later]
    o_ref = rest[n_later]
    later_bf16 = rest[n_later + 1:2 * n_later + 1]
    n_ref, proj_ref, kz_ref, vz_ref, st_ref, ao_ref, hy_ref = rest[2 * n_later + 1:]
    t = pl.program_id(1)
    win = KV_HALO + tt

    @pl.when(t == 0)
    def _():
        for ref in (kz_ref, vz_ref):
            ref[:, :, 0:KV_HALO, :] = jnp.zeros((ATT_KV_HEADS, 2, KV_HALO, ATT_KV_W), _BF16)
        st_ref[...] = jnp.zeros(st_ref.shape, _F32)

    n_ref[...] = _rms(x_ref[...], gmix_ref[...]).astype(_BF16)
    proj_ref[:, 0:_OFF_HQ] = _dot(n_ref[...], win_ref[:, 0:_OFF_HQ])
    proj_ref[:, _OFF_HQ:_OFF_GA] = _dot(n_ref[...], win_ref[:, _OFF_HQ:_OFF_GA])

    new = slice(KV_HALO, win)
    kvlane = lax.broadcasted_iota(jnp.int32, (tt, ATT_KV_W), 1)
    halves = (kvlane < ATT_HEAD_DIM, kvlane >= ATT_HEAD_DIM)
    for dst_ref, off in ((kz_ref, _OFF_AK), (vz_ref, _OFF_AV)):
        nat = proj_ref[:, off:off + ATT_KV_W]
        swapped = pltpu.roll(nat, ATT_HEAD_DIM, 1)
        for g in range(ATT_KV_HEADS):
            for par in range(2):
                dst_ref[g, par, new, :] = jnp.where(
                    halves[par], nat if g == par else swapped, 0.0).astype(_BF16)
    qi = lax.broadcasted_iota(jnp.int32, (ATT_QBLK, ATT_KBLK), 0)
    kj = lax.broadcasted_iota(jnp.int32, (ATT_QBLK, ATT_KBLK), 1)
    qc = lax.shift_right_logical(qi, CHUNK_LOG2)
    kc = lax.shift_right_logical(kj, CHUNK_LOG2)
    band = (kc >= qc) & (kc <= qc + LOOKBACK)
    absdist = jnp.abs(qi + KV_HALO - kj).astype(_F32)
    slopes = [2.0 ** (-8.0 * (h + 1) / ATT_HEADS) for h in range(ATT_HEADS)]
    alibi = [jnp.where(band, -(LOG2_E * slope) * absdist, -jnp.inf) for slope in slopes]
    q16 = (proj_ref[:, _OFF_AQ:_OFF_AQ + ATT_Q_W]
           * (LOG2_E * ATT_HEAD_DIM ** -0.5)).astype(_BF16)
    nblk = tt // ATT_QBLK
    probs = [[None] * ATT_HEADS for _ in range(nblk)]
    inv_den = [[None] * ATT_HEADS for _ in range(nblk)]
    for blk in range(nblk):
        r0 = blk * ATT_QBLK
        bias = ([jnp.where((t * tt + kj) >= KV_HALO, a, -jnp.inf) for a in alibi]
                if blk == 0 else alibi)
        for g in range(ATT_KV_HEADS):
            q2 = jnp.concatenate(
                [q16[r0:r0 + ATT_QBLK, 2 * ATT_HEAD_DIM * pair:2 * ATT_HEAD_DIM * (pair + 1)]
                 for pair in (2 * g, 2 * g + 1)], axis=0)
            for par in range(2):
                s2 = _dot_nt(q2, kz_ref[g, par, r0:r0 + ATT_KBLK, :])
                for half, h in enumerate((ATT_GROUP * g + par, ATT_GROUP * g + par + 2)):
                    s = s2[half * ATT_QBLK:(half + 1) * ATT_QBLK] + bias[h]
                    sink = sinks_ref[h] * LOG2_E
                    m = jnp.maximum(jnp.max(s, axis=-1, keepdims=True), sink)
                    p = jnp.exp2(s - m)
                    inv_den[blk][h] = 1.0 / (
                        jnp.sum(p, axis=-1, keepdims=True) + jnp.exp2(sink - m))
                    probs[blk][h] = p.astype(_BF16)

    proj_ref[:, _OFF_GA:IN_W] = _dot(n_ref[...], win_ref[:, _OFF_GA:IN_W])

    lraw = lbraw_ref[...]
    lexp = jnp.exp(lraw - jnp.max(lraw, axis=0, keepdims=True))
    lb = lexp[0:1, :] / jnp.sum(lexp, axis=0, keepdims=True)
    ri = lax.broadcasted_iota(jnp.int32, (HG_BLK, HG_BLK), 0)
    ci = lax.broadcasted_iota(jnp.int32, (HG_BLK, HG_BLK), 1)
    chunk_causal = ((lax.shift_right_logical(ri, CHUNK_LOG2)
                     == lax.shift_right_logical(ci, CHUNK_LOG2)) & (ri >= ci))
    rowchunk = lax.shift_right_logical(
        lax.broadcasted_iota(jnp.int32, (HG_BLK, HG_DIM), 0), CHUNK_LOG2)
    zero_hd = jnp.zeros((HG_BLK, HG_DIM), _BF16)
    ncb = HG_BLK // CHUNK
    hg_rows = [slice(r0, r0 + HG_BLK) for r0 in range(0, tt, HG_BLK)]

    def chunk_blocks(a):
        return jnp.concatenate([jnp.where(rowchunk == c, a, zero_hd) for c in range(ncb)], axis=1)

    def decayed_operands(rows):
        f = lb + (1.0 - lb) * jax.nn.sigmoid(proj_ref[rows, _OFF_HF:_OFF_HF + HG_W])
        b = _chunk_cumsum(jnp.log2(f))
        b_last = [b[(c + 1) * CHUNK - 1:(c + 1) * CHUNK, :] for c in range(ncb)]
        b_last_rows = jnp.concatenate(
            [jnp.broadcast_to(bl, (CHUNK, HG_W)) for bl in b_last], axis=0)
        q_dec = (proj_ref[rows, _OFF_HQ:_OFF_HQ + HG_W] * (HG_DIM ** -0.5)
                 * jnp.exp2(b)).astype(_BF16)
        k_inv = ((1.0 - f) * jnp.exp2(-b)).astype(_BF16)
        k_end = ((1.0 - f) * jnp.exp2(b_last_rows - b)).astype(_BF16)
        return q_dec, k_inv, k_end, [jnp.exp2(bl) for bl in b_last]

    operands = [decayed_operands(rows) for rows in hg_rows]

    olane = lax.broadcasted_iota(jnp.int32, (ATT_QBLK, 2 * ATT_HEAD_DIM), 1)
    for blk in range(nblk):
        r0 = blk * ATT_QBLK
        for g in range(ATT_KV_HEADS):
            h0 = ATT_GROUP * g
            o4 = (_dot(jnp.concatenate([probs[blk][h0], probs[blk][h0 + 2]], axis=0),
                       vz_ref[g, 0, r0:r0 + ATT_KBLK, :])
                  + _dot(jnp.concatenate([probs[blk][h0 + 1], probs[blk][h0 + 3]], axis=0),
                         vz_ref[g, 1, r0:r0 + ATT_KBLK, :]))
            for half in range(2):
                he, pair = h0 + 2 * half, 2 * g + half
                scale2 = jnp.where(olane < ATT_HEAD_DIM, inv_den[blk][he], inv_den[blk][he + 1])
                o2 = o4[half * ATT_QBLK:(half + 1) * ATT_QBLK]
                ao_ref[r0:r0 + ATT_QBLK, 2 * ATT_HEAD_DIM * pair:2 * ATT_HEAD_DIM * (pair + 1)] = (
                    (o2 * scale2).astype(_BF16))

    def state_free_dots(rows, q_dec, k_inv, k_end):
        a_mats, ds_all, v16 = [], [], []
        for hh in range(HG_HEADS):
            sl = slice(hh * HG_DIM, (hh + 1) * HG_DIM)
            a_mats.append(
                jnp.where(chunk_causal, _dot_nt(q_dec[:, sl], k_inv[:, sl]), 0.0).astype(_BF16))
            v = proj_ref[rows, _OFF_HI + hh * HG_DIM:_OFF_HI + (hh + 1) * HG_DIM]
            v16.append(v.astype(_BF16))
            ds_all.append(_dot(v.T.astype(_BF16), chunk_blocks(k_end[:, sl])))
        return a_mats, ds_all, v16

    local = [state_free_dots(rows, *ops[:3]) for rows, ops in zip(hg_rows, operands)]

    ya = _dot(ao_ref[...], wa_ref[...])

    gon = gon_ref[...]
    state = [st_ref[hh] for hh in range(HG_HEADS)]
    for rows, (q_dec, _, _, decay), (a_mats, ds_all, v16) in zip(hg_rows, operands, local):
        for hh in range(HG_HEADS):
            sl = slice(hh * HG_DIM, (hh + 1) * HG_DIM)
            st = state[hh]
            entering = []
            for c in range(ncb):
                entering.append(st.astype(_BF16))
                st = st * decay[c][:, sl] + ds_all[hh][:, c * HG_DIM:(c + 1) * HG_DIM]
            state[hh] = st
            o = (_dot(a_mats[hh], v16[hh])
                 + _dot_nt(chunk_blocks(q_dec[:, sl]), jnp.concatenate(entering, axis=1)))
            gate = jax.nn.silu(proj_ref[rows, _OFF_HG + hh * HG_DIM:_OFF_HG + (hh + 1) * HG_DIM])
            y = o * lax.rsqrt(jnp.mean(o * o, axis=-1, keepdims=True) + EPS) * gon * gate
            hy_ref[rows, sl] = y.astype(_BF16)
    for hh in range(HG_HEADS):
        st_ref[hh] = state[hh]

    yb = _dot(hy_ref[...], wb_ref[...])
    y = (jax.nn.sigmoid(proj_ref[:, _OFF_GA:_OFF_GA + D_MODEL]) * ya
         + jax.nn.sigmoid(proj_ref[:, _OFF_GB:_OFF_GB + D_MODEL]) * yb)
    h1 = x_ref[...] + _dot(y.astype(_BF16), wmix_ref[...])
    o_ref[...] = _cross_attend(h1, gc_ref[...], qk_ref, vo_ref)

    for ref in (kz_ref, vz_ref):
        ref[:, :, 0:KV_HALO, :] = ref[:, :, tt:tt + KV_HALO, :]
    _cast_slabs(later_f32, later_bf16)


def _const_spec(shape):
    return pl.BlockSpec(shape, lambda b, t: (0,) * len(shape), pipeline_mode=pl.Buffered(1))


def _slab_spec(shape, nsteps, linear_step):
    rows, cols = shape
    slab = next(s for s in range(BF16_ROWS, rows + 1, BF16_ROWS)
                if rows % s == 0 and rows // s <= nsteps)
    nslabs = rows // slab
    return pl.BlockSpec((slab, cols), lambda *ids: (linear_step(*ids) * nslabs // nsteps, 0))


def _cast_slabs(later_f32, later_bf16):
    for src, dst in zip(later_f32, later_bf16):
        dst[...] = src[...].astype(_BF16)


def _mixer(x, g_mix, w_in, lb_raw, sinks, g_onorm, w_a, w_b, w_mix, g_cross, qk, vo,
           later_weights, tt):
    bsz, seq, d = x.shape
    nt = seq // tt
    tile = pl.BlockSpec((None, tt, d), lambda b, t: (b, t, 0))
    slabs = [_slab_spec(w.shape, bsz * nt, lambda b, t: b * nt + t) for w in later_weights]
    h, *later16 = pl.pallas_call(
        functools.partial(_mixer_kernel, tt=tt, n_later=len(later_weights)),
        grid=(bsz, nt),
        in_specs=[
            tile,
            _const_spec(g_mix.shape),
            _const_spec(w_in.shape),
            _const_spec(lb_raw.shape),
            pl.BlockSpec(memory_space=pltpu.SMEM),
            _const_spec(g_onorm.shape),
            _const_spec(w_a.shape),
            _const_spec(w_b.shape),
            _const_spec(w_mix.shape),
            _const_spec(g_cross.shape),
            pl.BlockSpec((None,) + qk.shape[1:], lambda b, t: (b, 0, 0)),
            pl.BlockSpec((None,) + vo.shape[1:], lambda b, t: (b, 0, 0)),
        ] + slabs,
        out_specs=[tile] + slabs,
        out_shape=[jax.ShapeDtypeStruct(x.shape, _F32)]
        + [jax.ShapeDtypeStruct(w.shape, _BF16) for w in later_weights],
        scratch_shapes=[
            pltpu.VMEM((tt, d), _BF16),
            pltpu.VMEM((tt, IN_W), _F32),
            pltpu.VMEM((ATT_KV_HEADS, 2, KV_HALO + tt, ATT_KV_W), _BF16),
            pltpu.VMEM((ATT_KV_HEADS, 2, KV_HALO + tt, ATT_KV_W), _BF16),
            pltpu.VMEM((HG_HEADS, HG_DIM, HG_DIM), _F32),
            pltpu.VMEM((tt, ATT_Q_W), _BF16),
            pltpu.VMEM((tt, HG_W), _BF16),
        ],
        compiler_params=pltpu.CompilerParams(
            dimension_semantics=("arbitrary", "arbitrary"),
            vmem_limit_bytes=VMEM_LIMIT_BYTES),
        name="mixer",
    )(x, g_mix, w_in, lb_raw, sinks, g_onorm, w_a, w_b, w_mix, g_cross, qk, vo, *later_weights)
    return h, later16


def _mem_qk_vo_kernel(mem_ref, gmem_ref, wkv_ref, wq_ref, wo_ref, *rest, n_later, m):
    later_f32 = rest[:n_later]
    qk_ref, vo_ref = rest[n_later:n_later + 2]
    later_bf16 = rest[n_later + 2:2 * n_later + 2]
    wkv16_ref, wq16_ref, wo16_ref = rest[2 * n_later + 2:]

    @pl.when(pl.program_id(0) == 0)
    def _():
        wkv16_ref[...] = wkv_ref[...].astype(_BF16)
        wq16_ref[...] = wq_ref[...].astype(_BF16)
        wo16_ref[...] = wo_ref[...].astype(_BF16)

    nm = _rms(mem_ref[...], gmem_ref[...]).astype(_BF16)
    kv = _dot(nm, wkv16_ref[...]).astype(_BF16)
    for hd in range(X_HEADS):
        dims = slice(hd * X_HEAD_DIM, (hd + 1) * X_HEAD_DIM)
        vdims = slice(D_MODEL + hd * X_HEAD_DIM, D_MODEL + (hd + 1) * X_HEAD_DIM)
        mems = slice(hd * m, (hd + 1) * m)
        qk_ref[:, mems] = _dot_nt(wq16_ref[:, dims], kv[:, dims]).astype(_BF16)
        vo_ref[mems, :] = _dot(kv[:, vdims], wo16_ref[dims, :]).astype(_BF16)
    _cast_slabs(later_f32, later_bf16)


def _mem_qk_vo(mem, g_mem, w_ckv, w_cq, w_co, later_weights):
    bsz, m, d = mem.shape
    slabs = [_slab_spec(w.shape, bsz, lambda b: b) for w in later_weights]
    resident = lambda w: pl.BlockSpec(w.shape, lambda b: (0, 0), pipeline_mode=pl.Buffered(1))
    qk, vo, *later16 = pl.pallas_call(
        functools.partial(_mem_qk_vo_kernel, n_later=len(later_weights), m=m),
        grid=(bsz,),
        in_specs=[
            pl.BlockSpec((None, m, d), lambda b: (b, 0, 0)),
            pl.BlockSpec(g_mem.shape, lambda b: (0, 0)),
            resident(w_ckv), resident(w_cq), resident(w_co),
        ] + slabs,
        out_specs=[pl.BlockSpec((None, d, X_HEADS * m), lambda b: (b, 0, 0)),
                   pl.BlockSpec((None, X_HEADS * m, d), lambda b: (b, 0, 0))] + slabs,
        out_shape=[jax.ShapeDtypeStruct((bsz, d, X_HEADS * m), _BF16),
                   jax.ShapeDtypeStruct((bsz, X_HEADS * m, d), _BF16)]
        + [jax.ShapeDtypeStruct(w.shape, _BF16) for w in later_weights],
        scratch_shapes=[pltpu.VMEM(w.shape, _BF16) for w in (w_ckv, w_cq, w_co)],
        compiler_params=pltpu.CompilerParams(
            dimension_semantics=("arbitrary",), vmem_limit_bytes=VMEM_LIMIT_BYTES),
        name="mem_qk_vo",
    )(mem, g_mem, w_ckv, w_cq, w_co, *later_weights)
    return qk, vo, later16


def _shift_rows(a, first_row):
    rolled = pltpu.roll(a, 1, 0)
    row = lax.broadcasted_iota(jnp.int32, (SUBLANES, a.shape[1]), 0)
    head = jnp.where(row == 0, first_row, rolled[0:SUBLANES])
    return jnp.concatenate([head, rolled[SUBLANES:]], axis=0)


def _ffn_kernel(h_ref, gf_ref, win_ref, cw_ref, cb_ref, wd_ref, gfin_ref, o_ref,
                tail_ref, n_ref, act_ref, *, tt):
    t = pl.program_id(1)

    @pl.when(t == 0)
    def _():
        tail_ref[...] = jnp.zeros(tail_ref.shape, _F32)

    n_ref[...] = _rms(h_ref[...], gf_ref[...]).astype(_BF16)
    for c0 in range(0, D_FF, FF_COLS):
        cols = slice(c0, c0 + FF_COLS)
        u = _dot(n_ref[...], win_ref[:, cols])
        gate = _dot(n_ref[...], win_ref[:, D_FF + c0:D_FF + c0 + FF_COLS])
        w0, w1, w2 = cw_ref[0:1, cols], cw_ref[1:2, cols], cw_ref[2:3, cols]
        prev1 = tail_ref[SUBLANES - 1:SUBLANES, cols]
        prev2 = tail_ref[SUBLANES - 2:SUBLANES - 1, cols]
        acc = w1 * u + _shift_rows(w0 * u, w0 * prev1)
        conv = w2 * u + _shift_rows(acc, w1 * prev1 + w0 * prev2) + cb_ref[:, cols]
        tail_ref[:, cols] = u[tt - SUBLANES:tt, :]
        act_ref[:, cols] = (jax.nn.silu(conv) * gate).astype(_BF16)
    for r0 in range(0, tt, FF_OUT_ROWS):
        rows = slice(r0, r0 + FF_OUT_ROWS)
        h3 = h_ref[rows, :] + _dot(act_ref[rows, :], wd_ref[...])
        o_ref[rows, :] = _rms(h3, gfin_ref[...])


def _ffn(h, g_ffn, w_ffn_in, conv_w, conv_b, w_down, g_final, tt):
    bsz, seq, d = h.shape
    tile = pl.BlockSpec((None, tt, d), lambda b, t: (b, t, 0))
    return pl.pallas_call(
        functools.partial(_ffn_kernel, tt=tt),
        grid=(bsz, seq // tt),
        in_specs=[
            tile,
            _const_spec(g_ffn.shape),
            _const_spec(w_ffn_in.shape),
            _const_spec(conv_w.shape),
            _const_spec(conv_b.shape),
            _const_spec(w_down.shape),
            _const_spec(g_final.shape),
        ],
        out_specs=tile,
        out_shape=jax.ShapeDtypeStruct(h.shape, _F32),
        scratch_shapes=[
            pltpu.VMEM((SUBLANES, D_FF), _F32),
            pltpu.VMEM((tt, d), _BF16),
            pltpu.VMEM((tt, D_FF), _BF16),
        ],
        compiler_params=pltpu.CompilerParams(
            dimension_semantics=("arbitrary", "arbitrary"),
            vmem_limit_bytes=VMEM_LIMIT_BYTES),
        name="ffn",
    )(h, g_ffn, w_ffn_in, conv_w, conv_b, w_down, g_final)


def kernel(x, mem, g_mix, w_in, lower_bounds, attn_sinks, g_onorm, w_branch_a, w_branch_b,
           w_mix_out, g_cross, g_mem, w_cq, w_ckv, w_co, g_ffn, w_ffn_in, conv_w, conv_b,
           w_ffn_down, g_final):
    depth = g_mix.shape[0]
    assert depth == 1 and x.shape[-1] == D_MODEL and x.shape[1] % TOKEN_TILE == 0
    tt = TOKEN_TILE
    h = x
    for l in range(depth):
        qk, vo, (in16, a16, b16, mix16) = _mem_qk_vo(
            mem, g_mem[l][None], w_ckv[l], w_cq[l], w_co[l],
            [w_in[l], w_branch_a[l], w_branch_b[l], w_mix_out[l]])
        h, (ffn_in16, ffn_down16) = _mixer(
            h, g_mix[l][None], in16, lower_bounds.astype(_F32), attn_sinks[l], g_onorm[l][None],
            a16, b16, mix16, g_cross[l][None], qk, vo, [w_ffn_in[l], w_ffn_down[l]], tt)
        h = _ffn(h, g_ffn[l][None], ffn_in16, conv_w[l], conv_b[l][None], ffn_down16,
                 g_final[None], FFN_TILE)
    return h
```

```python
import functools

import jax
import jax.numpy as jnp
from jax import lax
from jax.experimental import pallas as pl
from jax.experimental.pallas import tpu as pltpu

D_MODEL = 1024
CHUNK = 64
CHUNK_LOG2 = CHUNK.bit_length() - 1
EPS = 1e-6
LOG2_E = 1.4426950408889634

ATT_HEADS = 8
ATT_KV_HEADS = 2
ATT_HEAD_DIM = 64
ATT_GROUP = ATT_HEADS // ATT_KV_HEADS
LOOKBACK = 2
ATT_Q_W = ATT_HEADS * ATT_HEAD_DIM
ATT_KV_W = ATT_KV_HEADS * ATT_HEAD_DIM
KV_HALO = LOOKBACK * CHUNK
ATT_QBLK = 2 * CHUNK
ATT_KBLK = ATT_QBLK + KV_HALO

HG_HEADS = 4
HG_DIM = 128
HG_W = HG_HEADS * HG_DIM
HG_BLK = 2 * CHUNK

X_HEADS = 4
X_HEAD_DIM = D_MODEL // X_HEADS

D_FF = 2816
CONV_WIDTH = 3
FF_COLS = 256
FF_OUT_ROWS = 256
SUBLANES = 8
BF16_ROWS = 16

_OFF_AQ = 0
_OFF_AK = _OFF_AQ + ATT_Q_W
_OFF_AV = _OFF_AK + ATT_KV_W
_OFF_HQ = _OFF_AV + ATT_KV_W
_OFF_HF = _OFF_HQ + HG_W
_OFF_HI = _OFF_HF + HG_W
_OFF_HG = _OFF_HI + HG_W
_OFF_GA = _OFF_HG + HG_W
_OFF_GB = _OFF_GA + D_MODEL
IN_W = _OFF_GB + D_MODEL

TOKEN_TILE = 512
CROSS_TILE = 1024
FFN_TILE = 1024
VMEM_LIMIT_BYTES = 56 * 1024 * 1024

_F32 = jnp.float32
_BF16 = jnp.bfloat16
_NT = (((1,), (1,)), ((), ()))


def _rms(xf, gain):
    return xf * lax.rsqrt(jnp.mean(xf * xf, axis=-1, keepdims=True) + EPS) * gain


def _dot(a, b):
    return jnp.dot(a, b, preferred_element_type=_F32)


def _dot_nt(a, b):
    return lax.dot_general(a, b, _NT, preferred_element_type=_F32)


def _chunk_cumsum(a):
    rows, cols = a.shape
    sub = lax.broadcasted_iota(jnp.int32, (SUBLANES, cols), 0)
    groups = []
    for g0 in range(0, rows, SUBLANES):
        g = a[g0:g0 + SUBLANES, :]
        step = 1
        while step < SUBLANES:
            g = g + jnp.where(sub >= step, pltpu.roll(g, step, 0), 0.0)
            step *= 2
        if g0 % CHUNK:
            g = g + carry
        carry = jnp.broadcast_to(g[SUBLANES - 1:SUBLANES, :], (SUBLANES, cols))
        groups.append(g)
    return jnp.concatenate(groups, axis=0)


def _mixer_kernel(x_ref, gmix_ref, win_ref, lbraw_ref, sinks_ref, gon_ref, wa_ref, wb_ref,
                  wmix_ref, *rest, tt, n_later):
    later_f32 = rest[:n_later]
    o_ref = rest[n_later]
    later_bf16 = rest[n_later + 1:2 * n_later + 1]
    n_ref, proj_ref, kz_ref, vz_ref, st_ref, ao_ref, hy_ref = rest[2 * n_later + 1:]
    t = pl.program_id(1)
    win = KV_HALO + tt

    @pl.when(t == 0)
    def _():
        for ref in (kz_ref, vz_ref):
            ref[:, :, 0:KV_HALO, :] = jnp.zeros((ATT_KV_HEADS, 2, KV_HALO, ATT_KV_W), _BF16)
        st_ref[...] = jnp.zeros(st_ref.shape, _F32)

    n_ref[...] = _rms(x_ref[...], gmix_ref[...]).astype(_BF16)
    proj_ref[:, 0:_OFF_HQ] = _dot(n_ref[...], win_ref[:, 0:_OFF_HQ])
    proj_ref[:, _OFF_HQ:_OFF_GA] = _dot(n_ref[...], win_ref[:, _OFF_HQ:_OFF_GA])

    new = slice(KV_HALO, win)
    kvlane = lax.broadcasted_iota(jnp.int32, (tt, ATT_KV_W), 1)
    halves = (kvlane < ATT_HEAD_DIM, kvlane >= ATT_HEAD_DIM)
    for dst_ref, off in ((kz_ref, _OFF_AK), (vz_ref, _OFF_AV)):
        nat = proj_ref[:, off:off + ATT_KV_W]
        swapped = pltpu.roll(nat, ATT_HEAD_DIM, 1)
        for g in range(ATT_KV_HEADS):
            for par in range(2):
                dst_ref[g, par, new, :] = jnp.where(
                    halves[par], nat if g == par else swapped, 0.0).astype(_BF16)
    qi = lax.broadcasted_iota(jnp.int32, (ATT_QBLK, ATT_KBLK), 0)
    kj = lax.broadcasted_iota(jnp.int32, (ATT_QBLK, ATT_KBLK), 1)
    qc = lax.shift_right_logical(qi, CHUNK_LOG2)
    kc = lax.shift_right_logical(kj, CHUNK_LOG2)
    band = (kc >= qc) & (kc <= qc + LOOKBACK)
    absdist = jnp.abs(qi + KV_HALO - kj).astype(_F32)
    slopes = [2.0 ** (-8.0 * (h + 1) / ATT_HEADS) for h in range(ATT_HEADS)]
    alibi = [jnp.where(band, -(LOG2_E * slope) * absdist, -jnp.inf) for slope in slopes]
    q16 = (proj_ref[:, _OFF_AQ:_OFF_AQ + ATT_Q_W]
           * (LOG2_E * ATT_HEAD_DIM ** -0.5)).astype(_BF16)
    nblk = tt // ATT_QBLK
    probs = [[None] * ATT_HEADS for _ in range(nblk)]
    inv_den = [[None] * ATT_HEADS for _ in range(nblk)]
    for blk in range(nblk):
        r0 = blk * ATT_QBLK
        bias = ([jnp.where((t * tt + kj) >= KV_HALO, a, -jnp.inf) for a in alibi]
                if blk == 0 else alibi)
        for g in range(ATT_KV_HEADS):
            q2 = jnp.concatenate(
                [q16[r0:r0 + ATT_QBLK, 2 * ATT_HEAD_DIM * pair:2 * ATT_HEAD_DIM * (pair + 1)]
                 for pair in (2 * g, 2 * g + 1)], axis=0)
            for par in range(2):
                s2 = _dot_nt(q2, kz_ref[g, par, r0:r0 + ATT_KBLK, :])
                for half, h in enumerate((ATT_GROUP * g + par, ATT_GROUP * g + par + 2)):
                    s = s2[half * ATT_QBLK:(half + 1) * ATT_QBLK] + bias[h]
                    sink = sinks_ref[h] * LOG2_E
                    m = jnp.maximum(jnp.max(s, axis=-1, keepdims=True), sink)
                    p = jnp.exp2(s - m)
                    inv_den[blk][h] = 1.0 / (
                        jnp.sum(p, axis=-1, keepdims=True) + jnp.exp2(sink - m))
                    probs[blk][h] = p.astype(_BF16)

    proj_ref[:, _OFF_GA:IN_W] = _dot(n_ref[...], win_ref[:, _OFF_GA:IN_W])

    lraw = lbraw_ref[...]
    lexp = jnp.exp(lraw - jnp.max(lraw, axis=0, keepdims=True))
    lb = lexp[0:1, :] / jnp.sum(lexp, axis=0, keepdims=True)
    ri = lax.broadcasted_iota(jnp.int32, (HG_BLK, HG_BLK), 0)
    ci = lax.broadcasted_iota(jnp.int32, (HG_BLK, HG_BLK), 1)
    chunk_causal = ((lax.shift_right_logical(ri, CHUNK_LOG2)
                     == lax.shift_right_logical(ci, CHUNK_LOG2)) & (ri >= ci))
    rowchunk = lax.shift_right_logical(
        lax.broadcasted_iota(jnp.int32, (HG_BLK, HG_DIM), 0), CHUNK_LOG2)
    zero_hd = jnp.zeros((HG_BLK, HG_DIM), _BF16)
    ncb = HG_BLK // CHUNK
    hg_rows = [slice(r0, r0 + HG_BLK) for r0 in range(0, tt, HG_BLK)]

    def chunk_blocks(a):
        return jnp.concatenate([jnp.where(rowchunk == c, a, zero_hd) for c in range(ncb)], axis=1)

    def decayed_operands(rows):
        f = lb + (1.0 - lb) * jax.nn.sigmoid(proj_ref[rows, _OFF_HF:_OFF_HF + HG_W])
        b = _chunk_cumsum(jnp.log2(f))
        b_last = [b[(c + 1) * CHUNK - 1:(c + 1) * CHUNK, :] for c in range(ncb)]
        b_last_rows = jnp.concatenate(
            [jnp.broadcast_to(bl, (CHUNK, HG_W)) for bl in b_last], axis=0)
        q_dec = (proj_ref[rows, _OFF_HQ:_OFF_HQ + HG_W] * (HG_DIM ** -0.5)
                 * jnp.exp2(b)).astype(_BF16)
        k_inv = ((1.0 - f) * jnp.exp2(-b)).astype(_BF16)
        k_end = ((1.0 - f) * jnp.exp2(b_last_rows - b)).astype(_BF16)
        return q_dec, k_inv, k_end, [jnp.exp2(bl) for bl in b_last]

    operands = [decayed_operands(rows) for rows in hg_rows]

    olane = lax.broadcasted_iota(jnp.int32, (ATT_QBLK, 2 * ATT_HEAD_DIM), 1)
    for blk in range(nblk):
        r0 = blk * ATT_QBLK
        for g in range(ATT_KV_HEADS):
            h0 = ATT_GROUP * g
            o4 = (_dot(jnp.concatenate([probs[blk][h0], probs[blk][h0 + 2]], axis=0),
                       vz_ref[g, 0, r0:r0 + ATT_KBLK, :])
                  + _dot(jnp.concatenate([probs[blk][h0 + 1], probs[blk][h0 + 3]], axis=0),
                         vz_ref[g, 1, r0:r0 + ATT_KBLK, :]))
            for half in range(2):
                he, pair = h0 + 2 * half, 2 * g + half
                scale2 = jnp.where(olane < ATT_HEAD_DIM, inv_den[blk][he], inv_den[blk][he + 1])
                o2 = o4[half * ATT_QBLK:(half + 1) * ATT_QBLK]
                ao_ref[r0:r0 + ATT_QBLK, 2 * ATT_HEAD_DIM * pair:2 * ATT_HEAD_DIM * (pair + 1)] = (
                    (o2 * scale2).astype(_BF16))

    def state_free_dots(rows, q_dec, k_inv, k_end):
        a_mats, ds_all, v16 = [], [], []
        for hh in range(HG_HEADS):
            sl = slice(hh * HG_DIM, (hh + 1) * HG_DIM)
            a_mats.append(
                jnp.where(chunk_causal, _dot_nt(q_dec[:, sl], k_inv[:, sl]), 0.0).astype(_BF16))
            v = proj_ref[rows, _OFF_HI + hh * HG_DIM:_OFF_HI + (hh + 1) * HG_DIM]
            v16.append(v.astype(_BF16))
            ds_all.append(_dot(v.T.astype(_BF16), chunk_blocks(k_end[:, sl])))
        return a_mats, ds_all, v16

    local = [state_free_dots(rows, *ops[:3]) for rows, ops in zip(hg_rows, operands)]

    ya = _dot(ao_ref[...], wa_ref[...])

    gon = gon_ref[...]
    state = [st_ref[hh] for hh in range(HG_HEADS)]
    for rows, (q_dec, _, _, decay), (a_mats, ds_all, v16) in zip(hg_rows, operands, local):
        for hh in range(HG_HEADS):
            sl = slice(hh * HG_DIM, (hh + 1) * HG_DIM)
            st = state[hh]
            entering = []
            for c in range(ncb):
                entering.append(st.astype(_BF16))
                st = st * decay[c][:, sl] + ds_all[hh][:, c * HG_DIM:(c + 1) * HG_DIM]
            state[hh] = st
            o = (_dot(a_mats[hh], v16[hh])
                 + _dot_nt(chunk_blocks(q_dec[:, sl]), jnp.concatenate(entering, axis=1)))
            gate = jax.nn.silu(proj_ref[rows, _OFF_HG + hh * HG_DIM:_OFF_HG + (hh + 1) * HG_DIM])
            y = o * lax.rsqrt(jnp.mean(o * o, axis=-1, keepdims=True) + EPS) * gon * gate
            hy_ref[rows, sl] = y.astype(_BF16)
    for hh in range(HG_HEADS):
        st_ref[hh] = state[hh]

    yb = _dot(hy_ref[...], wb_ref[...])
    y = (jax.nn.sigmoid(proj_ref[:, _OFF_GA:_OFF_GA + D_MODEL]) * ya
         + jax.nn.sigmoid(proj_ref[:, _OFF_GB:_OFF_GB + D_MODEL]) * yb)
    o_ref[...] = x_ref[...] + _dot(y.astype(_BF16), wmix_ref[...])

    for ref in (kz_ref, vz_ref):
        ref[:, :, 0:KV_HALO, :] = ref[:, :, tt:tt + KV_HALO, :]
    _cast_slabs(later_f32, later_bf16)


def _const_spec(shape):
    return pl.BlockSpec(shape, lambda b, t: (0,) * len(shape), pipeline_mode=pl.Buffered(1))


def _slab_spec(shape, nsteps, linear_step):
    rows, cols = shape
    slab = next(s for s in range(BF16_ROWS, rows + 1, BF16_ROWS)
                if rows % s == 0 and rows // s <= nsteps)
    nslabs = rows // slab
    return pl.BlockSpec((slab, cols), lambda *ids: (linear_step(*ids) * nslabs // nsteps, 0))


def _cast_slabs(later_f32, later_bf16):
    for src, dst in zip(later_f32, later_bf16):
        dst[...] = src[...].astype(_BF16)


def _mixer(x, g_mix, w_in, lb_raw, sinks, g_onorm, w_a, w_b, w_mix, later_weights, tt):
    bsz, seq, d = x.shape
    nt = seq // tt
    tile = pl.BlockSpec((None, tt, d), lambda b, t: (b, t, 0))
    slabs = [_slab_spec(w.shape, bsz * nt, lambda b, t: b * nt + t) for w in later_weights]
    h, *later16 = pl.pallas_call(
        functools.partial(_mixer_kernel, tt=tt, n_later=len(later_weights)),
        grid=(bsz, nt),
        in_specs=[
            tile,
            _const_spec(g_mix.shape),
            _const_spec(w_in.shape),
            _const_spec(lb_raw.shape),
            pl.BlockSpec(memory_space=pltpu.SMEM),
            _const_spec(g_onorm.shape),
            _const_spec(w_a.shape),
            _const_spec(w_b.shape),
            _const_spec(w_mix.shape),
        ] + slabs,
        out_specs=[tile] + slabs,
        out_shape=[jax.ShapeDtypeStruct(x.shape, _F32)]
        + [jax.ShapeDtypeStruct(w.shape, _BF16) for w in later_weights],
        scratch_shapes=[
            pltpu.VMEM((tt, d), _BF16),
            pltpu.VMEM((tt, IN_W), _F32),
            pltpu.VMEM((ATT_KV_HEADS, 2, KV_HALO + tt, ATT_KV_W), _BF16),
            pltpu.VMEM((ATT_KV_HEADS, 2, KV_HALO + tt, ATT_KV_W), _BF16),
            pltpu.VMEM((HG_HEADS, HG_DIM, HG_DIM), _F32),
            pltpu.VMEM((tt, ATT_Q_W), _BF16),
            pltpu.VMEM((tt, HG_W), _BF16),
        ],
        compiler_params=pltpu.CompilerParams(
            dimension_semantics=("parallel", "arbitrary"),
            vmem_limit_bytes=VMEM_LIMIT_BYTES),
        name="mixer",
    )(x, g_mix, w_in, lb_raw, sinks, g_onorm, w_a, w_b, w_mix, *later_weights)
    return h, later16


def _mem_qk_vo_kernel(mem_ref, gmem_ref, wkv_ref, wq_ref, wo_ref, *rest, n_later, m):
    later_f32 = rest[:n_later]
    qk_ref, vo_ref = rest[n_later:n_later + 2]
    later_bf16 = rest[n_later + 2:2 * n_later + 2]
    wkv16_ref, wq16_ref, wo16_ref = rest[2 * n_later + 2:]

    @pl.when(pl.program_id(0) == 0)
    def _():
        wkv16_ref[...] = wkv_ref[...].astype(_BF16)
        wq16_ref[...] = wq_ref[...].astype(_BF16)
        wo16_ref[...] = wo_ref[...].astype(_BF16)

    nm = _rms(mem_ref[...], gmem_ref[...]).astype(_BF16)
    kv = _dot(nm, wkv16_ref[...]).astype(_BF16)
    for hd in range(X_HEADS):
        dims = slice(hd * X_HEAD_DIM, (hd + 1) * X_HEAD_DIM)
        vdims = slice(D_MODEL + hd * X_HEAD_DIM, D_MODEL + (hd + 1) * X_HEAD_DIM)
        mems = slice(hd * m, (hd + 1) * m)
        qk_ref[:, mems] = _dot_nt(wq16_ref[:, dims], kv[:, dims]).astype(_BF16)
        vo_ref[mems, :] = _dot(kv[:, vdims], wo16_ref[dims, :]).astype(_BF16)
    _cast_slabs(later_f32, later_bf16)


def _mem_qk_vo(mem, g_mem, w_ckv, w_cq, w_co, later_weights):
    bsz, m, d = mem.shape
    slabs = [_slab_spec(w.shape, bsz, lambda b: b) for w in later_weights]
    resident = lambda w: pl.BlockSpec(w.shape, lambda b: (0, 0), pipeline_mode=pl.Buffered(1))
    qk, vo, *later16 = pl.pallas_call(
        functools.partial(_mem_qk_vo_kernel, n_later=len(later_weights), m=m),
        grid=(bsz,),
        in_specs=[
            pl.BlockSpec((None, m, d), lambda b: (b, 0, 0)),
            pl.BlockSpec(g_mem.shape, lambda b: (0, 0)),
            resident(w_ckv), resident(w_cq), resident(w_co),
        ] + slabs,
        out_specs=[pl.BlockSpec((None, d, X_HEADS * m), lambda b: (b, 0, 0)),
                   pl.BlockSpec((None, X_HEADS * m, d), lambda b: (b, 0, 0))] + slabs,
        out_shape=[jax.ShapeDtypeStruct((bsz, d, X_HEADS * m), _BF16),
                   jax.ShapeDtypeStruct((bsz, X_HEADS * m, d), _BF16)]
        + [jax.ShapeDtypeStruct(w.shape, _BF16) for w in later_weights],
        scratch_shapes=[pltpu.VMEM(w.shape, _BF16) for w in (w_ckv, w_cq, w_co)],
        compiler_params=pltpu.CompilerParams(
            dimension_semantics=("arbitrary",), vmem_limit_bytes=VMEM_LIMIT_BYTES),
        name="mem_qk_vo",
    )(mem, g_mem, w_ckv, w_cq, w_co, *later_weights)
    return qk, vo, later16


def _cross_kernel(h_ref, gc_ref, qk_ref, vo_ref, o_ref, *, m):
    h = h_ref[...]
    s_all = _dot(_rms(h, gc_ref[...]).astype(_BF16), qk_ref[...]) * (LOG2_E * X_HEAD_DIM ** -0.5)
    probs = []
    for hd in range(X_HEADS):
        s = s_all[:, hd * m:(hd + 1) * m]
        p = jnp.exp2(s - jnp.max(s, axis=-1, keepdims=True))
        probs.append((p * (1.0 / jnp.sum(p, axis=-1, keepdims=True))).astype(_BF16))
    o_ref[...] = h + _dot(jnp.concatenate(probs, axis=-1), vo_ref[...])


def _cross(h, g_cross, qk, vo, tt):
    bsz, seq, d = h.shape
    m = vo.shape[1] // X_HEADS
    tile = pl.BlockSpec((None, tt, d), lambda b, t: (b, t, 0))
    return pl.pallas_call(
        functools.partial(_cross_kernel, m=m),
        grid=(bsz, seq // tt),
        in_specs=[
            tile,
            _const_spec(g_cross.shape),
            pl.BlockSpec((None,) + qk.shape[1:], lambda b, t: (b, 0, 0)),
            pl.BlockSpec((None,) + vo.shape[1:], lambda b, t: (b, 0, 0)),
        ],
        out_specs=tile,
        out_shape=jax.ShapeDtypeStruct(h.shape, _F32),
        compiler_params=pltpu.CompilerParams(
            dimension_semantics=("parallel", "arbitrary"),
            vmem_limit_bytes=VMEM_LIMIT_BYTES),
        name="cross",
    )(h, g_cross, qk, vo)


def _shift_rows(a, first_row):
    rolled = pltpu.roll(a, 1, 0)
    row = lax.broadcasted_iota(jnp.int32, (SUBLANES, a.shape[1]), 0)
    head = jnp.where(row == 0, first_row, rolled[0:SUBLANES])
    return jnp.concatenate([head, rolled[SUBLANES:]], axis=0)


def _ffn_kernel(h_ref, gf_ref, win_ref, cw_ref, cb_ref, wd_ref, gfin_ref, o_ref,
                tail_ref, n_ref, act_ref, *, tt):
    t = pl.program_id(1)

    @pl.when(t == 0)
    def _():
        tail_ref[...] = jnp.zeros(tail_ref.shape, _F32)

    n_ref[...] = _rms(h_ref[...], gf_ref[...]).astype(_BF16)
    for c0 in range(0, D_FF, FF_COLS):
        cols = slice(c0, c0 + FF_COLS)
        u = _dot(n_ref[...], win_ref[:, cols])
        gate = _dot(n_ref[...], win_ref[:, D_FF + c0:D_FF + c0 + FF_COLS])
        w0, w1, w2 = cw_ref[0:1, cols], cw_ref[1:2, cols], cw_ref[2:3, cols]
        prev1 = tail_ref[SUBLANES - 1:SUBLANES, cols]
        prev2 = tail_ref[SUBLANES - 2:SUBLANES - 1, cols]
        acc = w1 * u + _shift_rows(w0 * u, w0 * prev1)
        conv = w2 * u + _shift_rows(acc, w1 * prev1 + w0 * prev2) + cb_ref[:, cols]
        tail_ref[:, cols] = u[tt - SUBLANES:tt, :]
        act_ref[:, cols] = (jax.nn.silu(conv) * gate).astype(_BF16)
    for r0 in range(0, tt, FF_OUT_ROWS):
        rows = slice(r0, r0 + FF_OUT_ROWS)
        h3 = h_ref[rows, :] + _dot(act_ref[rows, :], wd_ref[...])
        o_ref[rows, :] = _rms(h3, gfin_ref[...])


def _ffn(h, g_ffn, w_ffn_in, conv_w, conv_b, w_down, g_final, tt):
    bsz, seq, d = h.shape
    tile = pl.BlockSpec((None, tt, d), lambda b, t: (b, t, 0))
    return pl.pallas_call(
        functools.partial(_ffn_kernel, tt=tt),
        grid=(bsz, seq // tt),
        in_specs=[
            tile,
            _const_spec(g_ffn.shape),
            _const_spec(w_ffn_in.shape),
            _const_spec(conv_w.shape),
            _const_spec(conv_b.shape),
            _const_spec(w_down.shape),
            _const_spec(g_final.shape),
        ],
        out_specs=tile,
        out_shape=jax.ShapeDtypeStruct(h.shape, _F32),
        scratch_shapes=[
            pltpu.VMEM((SUBLANES, D_FF), _F32),
            pltpu.VMEM((tt, d), _BF16),
            pltpu.VMEM((tt, D_FF), _BF16),
        ],
        compiler_params=pltpu.CompilerParams(
            dimension_semantics=("parallel", "arbitrary"),
            vmem_limit_bytes=VMEM_LIMIT_BYTES),
        name="ffn",
    )(h, g_ffn, w_ffn_in, conv_w, conv_b, w_down, g_final)


def kernel(x, mem, g_mix, w_in, lower_bounds, attn_sinks, g_onorm, w_branch_a, w_branch_b,
           w_mix_out, g_cross, g_mem, w_cq, w_ckv, w_co, g_ffn, w_ffn_in, conv_w, conv_b,
           w_ffn_down, g_final):
    depth = g_mix.shape[0]
    assert depth == 1 and x.shape[-1] == D_MODEL and x.shape[1] % TOKEN_TILE == 0
    tt = TOKEN_TILE
    h = x
    for l in range(depth):
        qk, vo, (in16, a16, b16, mix16) = _mem_qk_vo(
            mem, g_mem[l][None], w_ckv[l], w_cq[l], w_co[l],
            [w_in[l], w_branch_a[l], w_branch_b[l], w_mix_out[l]])
        h, (ffn_in16, ffn_down16) = _mixer(
            h, g_mix[l][None], in16, lower_bounds.astype(_F32), attn_sinks[l], g_onorm[l][None],
            a16, b16, mix16, [w_ffn_in[l], w_ffn_down[l]], tt)
        h = _cross(h, g_cross[l][None], qk, vo, CROSS_TILE)
        h = _ffn(h, g_ffn[l][None], ffn_in16, conv_w[l], conv_b[l][None], ffn_down16,
                 g_final[None], FFN_TILE)
    return h
```
